```python
import jax, jax.numpy as jnp
from jax import lax
import numpy as np

D_MODEL = 2048
BATCH = 1
SEQ = 16384
DEPTH = 4

HEAD_DIM = 128
ROT_DIM = HEAD_DIM // 4
ROPE_THETA = 500000.0
DIL_GROUPS = ((128, 1), (512, 4), (2048, 16))
N_DIL = len(DIL_GROUPS)
A_HEADS = 4
A_WIDTH = A_HEADS * HEAD_DIM
SGU_CHUNK = 128
SGU_GROUPS = 4
SGU_GROUP_DIM = 128
B_WIDTH = SGU_GROUPS * SGU_GROUP_DIM
C_HEADS = 4
C_HEAD_DIM = 256
C_WIDTH = C_HEADS * C_HEAD_DIM
C_CHUNK = 128
C_CONV = 4
N_BRANCH = 3
D_FF = 5632
FFN_CONV = 3
EPS = 1e-6

A_COLS = 3 * N_DIL * A_WIDTH
B_COLS = 2 * B_WIDTH
C_COLS = 4 * C_WIDTH + 2 * C_HEADS
G_COLS = N_BRANCH * D_MODEL
N_IN = A_COLS + B_COLS + C_COLS + G_COLS

kernel_name = 'hybrid_dilated_sgu_mlstm_gated_trunk'


def rms_norm(x, g):
    xf = x.astype(jnp.float32)
    y = xf * lax.rsqrt(jnp.mean(xf * xf, axis=-1, keepdims=True) + EPS)
    return (y * g.astype(jnp.float32)).astype(x.dtype)


def rope_tables(S):
    half = ROT_DIM // 2
    inv = jnp.power(jnp.float32(ROPE_THETA), -jnp.arange(half, dtype=jnp.float32) * (2.0 / ROT_DIM))
    ang = jnp.arange(S, dtype=jnp.float32)[:, None] * inv[None, :]
    return jnp.cos(ang), jnp.sin(ang)


def partial_rope(t, cos, sin):
    half = ROT_DIM // 2
    c = cos[None, :, None, None, :]
    s = sin[None, :, None, None, :]
    tr = t[..., :ROT_DIM].astype(jnp.float32)
    x1, x2 = tr[..., :half], tr[..., half:]
    rot = jnp.concatenate([x1 * c - x2 * s, x2 * c + x1 * s], axis=-1)
    return jnp.concatenate([rot.astype(t.dtype), t[..., ROT_DIM:]], axis=-1)


def causal_dwconv(x, w, b):
    K = w.shape[0]
    y = lax.conv_general_dilated(x, w[:, None, :].astype(x.dtype), window_strides=(1,),
                                 padding=[(K - 1, 0)], dimension_numbers=('NWC', 'WIO', 'NWC'),
                                 feature_group_count=x.shape[-1])
    return y + b.astype(x.dtype)


def dilated_attention(q, k, v, window, dilation):
    B, S, H, dh = q.shape
    blk = window // dilation
    span = dilation * blk
    P = -(-S // span) * span
    L = P // dilation
    nb = L // blk

    def to_sub(t):
        t = jnp.pad(t, ((0, 0), (0, P - S), (0, 0), (0, 0)))
        t = t.reshape(B, L, dilation, H, dh).transpose(0, 2, 1, 3, 4)
        return t.reshape(B, dilation, nb, blk, H, dh)

    def with_prev(t):
        prev = jnp.pad(t, ((0, 0), (0, 0), (1, 0), (0, 0), (0, 0), (0, 0)))[:, :, :-1]
        return jnp.concatenate([prev, t], axis=3)

    qs = to_sub(q)
    kb = with_prev(to_sub(k))
    vb = with_prev(to_sub(v))
    s = jnp.einsum('brnqhd,brnkhd->brnhqk', qs, kb, preferred_element_type=jnp.float32) * (dh ** -0.5)
    i = jnp.arange(blk)[:, None]
    kk = jnp.arange(2 * blk)[None, :]
    band = (kk >= i) & (kk <= i + blk)
    has_prev = (jnp.arange(nb) > 0)[:, None, None] | (kk >= blk)[None]
    mask = band[None] & has_prev
    s = jnp.where(mask[None, None, :, None], s, -jnp.inf)
    lse = jax.nn.logsumexp(s, axis=-1)
    p = jnp.exp(s - lse[..., None])
    o = jnp.einsum('brnhqk,brnkhd->brnqhd', p, vb.astype(jnp.float32))
    o = o.reshape(B, dilation, L, H, dh).transpose(0, 2, 1, 3, 4).reshape(B, P, H, dh)[:, :S]
    lse = lse.transpose(0, 1, 2, 4, 3).reshape(B, dilation, L, H).transpose(0, 2, 1, 3).reshape(B, P, H)[:, :S]
    return o, lse


def spatial_gate(u, v, g_norm, w_s, b_s):
    B, S, _ = u.shape
    v = rms_norm(v, g_norm)
    vc = v.reshape(B, S // SGU_CHUNK, SGU_CHUNK, SGU_GROUPS, SGU_GROUP_DIM)
    w = w_s * jnp.tril(jnp.ones((SGU_CHUNK, SGU_CHUNK), w_s.dtype))[None]
    mixed = jnp.einsum('gts,bnsgc->bntgc', w.astype(vc.dtype), vc) + b_s.T.astype(vc.dtype)[None, None, :, :, None]
    return u * mixed.reshape(B, S, B_WIDTH)


def mlstm(q, k, v, ig, fg):
    B, S, H, d = q.shape
    nc = S // C_CHUNK
    k = k * (d ** -0.5)
    lf = jax.nn.log_sigmoid(fg)

    def chunks(t):
        t = t.reshape((B, nc, C_CHUNK, H) + t.shape[3:])
        return jnp.moveaxis(t, (1, 3), (0, 2))

    causal = jnp.tril(jnp.ones((C_CHUNK, C_CHUNK), bool))

    def step(carry, inp):
        C, n, m = carry
        qc, kc, vc, ic, fc = inp
        b = jnp.cumsum(fc, axis=-1)
        Dm = jnp.where(causal, b[..., :, None] - b[..., None, :] + ic[..., None, :], -jnp.inf)
        m_t = jnp.maximum(b + m[..., None], jnp.max(Dm, axis=-1))
        wmat = jnp.exp(Dm - m_t[..., None])
        inter = jnp.exp(b + m[..., None] - m_t)
        sc = jnp.einsum('bhtd,bhsd->bhts', qc, kc) * wmat
        num = jnp.einsum('bhts,bhsd->bhtd', sc, vc) + inter[..., None] * jnp.einsum('bhtk,bhkv->bhtv', qc, C)
        den = jnp.sum(sc, axis=-1) + inter * jnp.einsum('bhtk,bhk->bht', qc, n)
        h = num / jnp.maximum(jnp.abs(den), jnp.exp(-m_t))[..., None]
        m_new = m_t[..., -1]
        wk = jnp.exp(b[..., -1:] - b + ic - m_new[..., None])
        decay = jnp.exp(b[..., -1] + m - m_new)
        C = decay[..., None, None] * C + jnp.einsum('bhs,bhsk,bhsv->bhkv', wk, kc, vc)
        n = decay[..., None] * n + jnp.einsum('bhs,bhsk->bhk', wk, kc)
        return (C, n, m_new), h

    init = (jnp.zeros((B, H, d, d), jnp.float32), jnp.zeros((B, H, d), jnp.float32), jnp.zeros((B, H), jnp.float32))
    _, hs = lax.scan(step, init, (chunks(q), chunks(k), chunks(v), chunks(ig), chunks(lf)))
    return jnp.moveaxis(hs, (0, 2), (1, 3)).reshape(B, S, H * d)


def setup_inputs(seed: int = 0) -> dict:
    key = jax.random.key(seed)
    ks = jax.random.split(key, 21)

    def nrm(k, shape, s):
        return jax.random.normal(k, shape, jnp.float32) * s

    res_scale = (2 * DEPTH) ** -0.5
    return {
        'x': nrm(ks[0], (BATCH, SEQ, D_MODEL), 1.0),
        'norm1_g': 1.0 + nrm(ks[1], (DEPTH, D_MODEL), 0.05),
        'w_in': nrm(ks[2], (DEPTH, D_MODEL, N_IN), D_MODEL ** -0.5),
        'conv_qk_w': nrm(ks[3], (DEPTH, C_CONV, 2 * C_WIDTH), C_CONV ** -0.5),
        'conv_qk_b': nrm(ks[4], (DEPTH, 2 * C_WIDTH), 0.02),
        'b_igate': nrm(ks[5], (DEPTH, C_HEADS), 0.1),
        'b_fgate': jnp.linspace(3.0, 6.0, C_HEADS, dtype=jnp.float32)[None, :] + nrm(ks[6], (DEPTH, C_HEADS), 0.1),
        'sgu_norm_g': 1.0 + nrm(ks[7], (DEPTH, B_WIDTH), 0.05),
        'sgu_w': nrm(ks[8], (DEPTH, SGU_GROUPS, SGU_CHUNK, SGU_CHUNK), SGU_CHUNK ** -0.5),
        'sgu_b': 1.0 + nrm(ks[9], (DEPTH, SGU_GROUPS, SGU_CHUNK), 0.02),
        'w_branch_a': nrm(ks[10], (DEPTH, A_WIDTH, D_MODEL), A_WIDTH ** -0.5),
        'w_branch_b': nrm(ks[11], (DEPTH, B_WIDTH, D_MODEL), B_WIDTH ** -0.5),
        'w_branch_c': nrm(ks[12], (DEPTH, C_WIDTH, D_MODEL), C_WIDTH ** -0.5),
        'w_out': nrm(ks[13], (DEPTH, D_MODEL, D_MODEL), D_MODEL ** -0.5 * res_scale),
        'norm2_g': 1.0 + nrm(ks[14], (DEPTH, D_MODEL), 0.05),
        'w_up': nrm(ks[15], (DEPTH, D_MODEL, 2 * D_FF), D_MODEL ** -0.5),
        'ffn_conv_w': nrm(ks[16], (DEPTH, FFN_CONV, 2 * D_FF), FFN_CONV ** -0.5),
        'ffn_conv_b': nrm(ks[17], (DEPTH, 2 * D_FF), 0.02),
        'w_down': nrm(ks[18], (DEPTH, D_FF, D_MODEL), D_FF ** -0.5 * res_scale),
        'final_norm_g': 1.0 + nrm(ks[19], (D_MODEL,), 0.05),
    }


def reference(x, norm1_g, w_in, conv_qk_w, conv_qk_b, b_igate, b_fgate, sgu_norm_g, sgu_w, sgu_b,
              w_branch_a, w_branch_b, w_branch_c, w_out, norm2_g, w_up, ffn_conv_w, ffn_conv_b,
              w_down, final_norm_g):
    B, S, D = x.shape
    cos, sin = rope_tables(S)
    splits = [A_COLS, A_COLS + B_COLS, A_COLS + B_COLS + C_COLS]
    for l in range(DEPTH):
        h = rms_norm(x, norm1_g[l])
        z = h @ w_in[l]
        za, zb, zc, zg = jnp.split(z, splits, axis=-1)

        qkv = za.reshape(B, S, 3, N_DIL, A_HEADS, HEAD_DIM)
        qa = partial_rope(qkv[:, :, 0], cos, sin)
        ka = partial_rope(qkv[:, :, 1], cos, sin)
        va = qkv[:, :, 2]
        outs, lses = [], []
        for g, (win, dil) in enumerate(DIL_GROUPS):
            o_g, lse_g = dilated_attention(qa[:, :, g], ka[:, :, g], va[:, :, g], win, dil)
            outs.append(o_g)
            lses.append(lse_g)
        wts = jax.nn.softmax(jnp.stack(lses, axis=0), axis=0)
        y_a = jnp.einsum('gbsh,gbshd->bshd', wts, jnp.stack(outs, axis=0)).reshape(B, S, A_WIDTH).astype(x.dtype)

        u, vb = jnp.split(jax.nn.gelu(zb), 2, axis=-1)
        y_b = spatial_gate(u, vb, sgu_norm_g[l], sgu_w[l], sgu_b[l])

        qk_c, v_c, o_c, gates_c = jnp.split(zc, [2 * C_WIDTH, 3 * C_WIDTH, 4 * C_WIDTH], axis=-1)
        qk_c = jax.nn.silu(causal_dwconv(qk_c, conv_qk_w[l], conv_qk_b[l]))
        q_c, k_c = jnp.split(qk_c, 2, axis=-1)
        gates_c = gates_c.astype(jnp.float32)
        ig = gates_c[..., :C_HEADS] + b_igate[l].astype(jnp.float32)
        fg = gates_c[..., C_HEADS:] + b_fgate[l].astype(jnp.float32)
        hs = (B, S, C_HEADS, C_HEAD_DIM)
        h_c = mlstm(q_c.reshape(hs).astype(jnp.float32), k_c.reshape(hs).astype(jnp.float32),
                    v_c.reshape(hs).astype(jnp.float32), ig, fg)
        y_c = (jax.nn.sigmoid(o_c.astype(jnp.float32)) * h_c).astype(x.dtype)

        gates = jax.nn.sigmoid(zg).reshape(B, S, N_BRANCH, D)
        merged = (gates[:, :, 0] * (y_a @ w_branch_a[l])
                  + gates[:, :, 1] * (y_b @ w_branch_b[l])
                  + gates[:, :, 2] * (y_c @ w_branch_c[l]))
        x = x + merged @ w_out[l]

        h2 = rms_norm(x, norm2_g[l])
        a = causal_dwconv(h2 @ w_up[l], ffn_conv_w[l], ffn_conv_b[l])
        a_gate, a_up = jnp.split(a, 2, axis=-1)
        x = x + (jax.nn.silu(a_gate) * a_up) @ w_down[l]
    return rms_norm(x, final_norm_g)
```

```python
import functools
import math

import jax
import jax.numpy as jnp
from jax import lax
from jax.experimental import pallas as pl
from jax.experimental.pallas import tpu as pltpu

BF = jnp.bfloat16
F32 = jnp.float32

D_MODEL = 2048
HEAD_DIM = 128
ROT_DIM = HEAD_DIM // 4
ROPE_THETA = 500000.0
DIL_GROUPS = ((128, 1), (512, 4), (2048, 16))
N_DIL = len(DIL_GROUPS)
A_HEADS = 4
A_WIDTH = A_HEADS * HEAD_DIM
ATT_BLK = 128
SGU_CHUNK = 128
SGU_GROUPS = 4
SGU_GROUP_DIM = 128
B_WIDTH = SGU_GROUPS * SGU_GROUP_DIM
C_HEADS = 4
C_HEAD_DIM = 256
C_WIDTH = C_HEADS * C_HEAD_DIM
C_CHUNK = 128
C_CONV = 4
N_BRANCH = 3
D_FF = 5632
FFN_CONV = 3
EPS = 1e-6

A_COLS = 3 * N_DIL * A_WIDTH
B_COLS = 2 * B_WIDTH
C_MAIN = 4 * C_WIDTH
C_GATES = 2 * C_HEADS
G_COLS = N_BRANCH * D_MODEL

LANE = 128
HALO = 8
VMEM_LIMIT = 56 * 1024 * 1024


def _params(sem):
    return pltpu.CompilerParams(dimension_semantics=sem, vmem_limit_bytes=VMEM_LIMIT)


def _sigmoid(x):
    return 1.0 / (1.0 + jnp.exp(-x))


def _rms(x, g):
    return x * lax.rsqrt(jnp.mean(x * x, axis=-1, keepdims=True) + EPS) * g


def _dot(a, b):
    return jnp.dot(a, b, preferred_element_type=F32)


def _rmsnorm_kernel(x_ref, g_ref, o_ref):
    o_ref[...] = _rms(x_ref[...], g_ref[...]).astype(o_ref.dtype)


def _rmsnorm(x, g, tm):
    S, D = x.shape
    return pl.pallas_call(
        _rmsnorm_kernel,
        grid=(S // tm,),
        in_specs=[pl.BlockSpec((tm, D), lambda m: (m, 0)), pl.BlockSpec((1, D), lambda m: (0, 0))],
        out_specs=pl.BlockSpec((tm, D), lambda m: (m, 0)),
        out_shape=jax.ShapeDtypeStruct((S, D), BF),
        compiler_params=_params(("parallel",)),
        name="rmsnorm",
    )(x, g)


def _inproj_a_kernel(h_ref, w_ref, cos_ref, sa_ref, sb_ref, o_ref):
    n = pl.program_id(1)
    acc = _dot(h_ref[...], w_ref[...])

    @pl.when(n >= 2 * N_DIL)
    def _():
        o_ref[...] = acc.astype(o_ref.dtype)

    @pl.when(n < 2 * N_DIL)
    def _():
        scale = jnp.where(n < N_DIL, HEAD_DIM ** -0.5, 1.0).astype(F32)
        c, sa, sb = cos_ref[...], sa_ref[...], sb_ref[...]
        for j in range(A_HEADS):
            xj = acc[:, j * HEAD_DIM:(j + 1) * HEAD_DIM]
            r = xj * c + pltpu.roll(xj, HEAD_DIM - ROT_DIM // 2, 1) * sa + pltpu.roll(xj, ROT_DIM // 2, 1) * sb
            o_ref[:, j * HEAD_DIM:(j + 1) * HEAD_DIM] = (r * scale).astype(o_ref.dtype)


def _inproj_a(h, wa, layer, cosf, sina, sinb, tm):
    S, D = h.shape
    tn = A_WIDTH
    return pl.pallas_call(
        _inproj_a_kernel,
        grid=(S // tm, A_COLS // tn),
        in_specs=[
            pl.BlockSpec((tm, D), lambda m, n: (m, 0)),
            pl.BlockSpec((None, D, tn), lambda m, n: (layer, 0, n)),
            pl.BlockSpec((tm, HEAD_DIM), lambda m, n: (m, 0)),
            pl.BlockSpec((tm, HEAD_DIM), lambda m, n: (m, 0)),
            pl.BlockSpec((tm, HEAD_DIM), lambda m, n: (m, 0)),
        ],
        out_specs=pl.BlockSpec((tm, tn), lambda m, n: (m, n)),
        out_shape=jax.ShapeDtypeStruct((S, A_COLS), BF),
        compiler_params=_params(("parallel", "arbitrary")),
        name="inproj_a",
    )(h, wa, cosf, sina, sinb)


def _attn_core(q_ref, kc_ref, kp_ref, vc_ref, vp_ref, n):
    i = lax.broadcasted_iota(jnp.int32, (ATT_BLK, 2 * ATT_BLK), 0)
    kk = lax.broadcasted_iota(jnp.int32, (ATT_BLK, 2 * ATT_BLK), 1)
    mask = (kk >= i) & (kk <= i + ATT_BLK) & ((kk >= ATT_BLK) | (n > 0))
    for hd in range(A_HEADS):
        sl = slice(hd * HEAD_DIM, (hd + 1) * HEAD_DIM)
        q = q_ref[:, sl]
        k = jnp.concatenate([kp_ref[:, sl], kc_ref[:, sl]], axis=0)
        v = jnp.concatenate([vp_ref[:, sl], vc_ref[:, sl]], axis=0)
        s = lax.dot_general(q, k, (((1,), (1,)), ((), ())), preferred_element_type=F32)
        s = jnp.where(mask, s, -jnp.inf)
        m = jnp.max(s, axis=-1, keepdims=True)
        p = jnp.exp(s - m)
        l = jnp.sum(p, axis=-1, keepdims=True)
        o = _dot(p.astype(BF), v) / l
        yield sl, o, m + jnp.log(l)


def _attn_group_kernel(q_ref, kc_ref, kp_ref, vc_ref, vp_ref, o_ref, lse_ref):
    n = pl.program_id(1)
    for sl, o, lse in _attn_core(q_ref, kc_ref, kp_ref, vc_ref, vp_ref, n):
        o_ref[:, sl] = o
        lse_ref[:, sl] = jnp.broadcast_to(lse, (ATT_BLK, HEAD_DIM))


def _attn_merge_kernel(q_ref, kc_ref, kp_ref, vc_ref, vp_ref, o2_ref, l2_ref, o3_ref, l3_ref, y_ref):
    n = pl.program_id(1)
    for sl, o1, lse1 in _attn_core(q_ref, kc_ref, kp_ref, vc_ref, vp_ref, n):
        l2, l3 = l2_ref[:, sl], l3_ref[:, sl]
        mx = jnp.maximum(jnp.maximum(lse1, l2), l3)
        w1, w2, w3 = jnp.exp(lse1 - mx), jnp.exp(l2 - mx), jnp.exp(l3 - mx)
        y = (w1 * o1 + w2 * o2_ref[:, sl] + w3 * o3_ref[:, sl]) / (w1 + w2 + w3)
        y_ref[:, sl] = y.astype(y_ref.dtype)


def _attention(qkv, S):
    nslab = A_COLS // A_WIDTH
    partial = {}
    for g in (2, 1, 0):
        _, d = DIL_GROUPS[g]
        L = S // d
        nb = L // ATT_BLK
        x2 = qkv.reshape(L, d * A_COLS)

        def spec(c, prev, d=d):
            if prev:
                return pl.BlockSpec((ATT_BLK, A_WIDTH), lambda r, n: (jnp.maximum(n - 1, 0), r * nslab + c))
            return pl.BlockSpec((ATT_BLK, A_WIDTH), lambda r, n: (n, r * nslab + c))

        qkv_specs = [spec(g, False), spec(N_DIL + g, False), spec(N_DIL + g, True),
                     spec(2 * N_DIL + g, False), spec(2 * N_DIL + g, True)]
        out_spec = pl.BlockSpec((ATT_BLK, A_WIDTH), lambda r, n: (n, r))
        if g > 0:
            o, lse = pl.pallas_call(
                _attn_group_kernel,
                grid=(d, nb),
                in_specs=qkv_specs,
                out_specs=[out_spec, out_spec],
                out_shape=[jax.ShapeDtypeStruct((L, d * A_WIDTH), F32)] * 2,
                compiler_params=_params(("parallel", "arbitrary")),
                name=f"attn_g{g}",
            )(x2, x2, x2, x2, x2)
            partial[g] = (o.reshape(S, A_WIDTH), lse.reshape(S, A_WIDTH))
        else:
            return pl.pallas_call(
                _attn_merge_kernel,
                grid=(d, nb),
                in_specs=qkv_specs + [out_spec] * 4,
                out_specs=out_spec,
                out_shape=jax.ShapeDtypeStruct((S, A_WIDTH), BF),
                compiler_params=_params(("parallel", "arbitrary")),
                name="attn_g0_merge",
            )(x2, x2, x2, x2, x2, partial[1][0], partial[1][1], partial[2][0], partial[2][1])


def _sgu_kernel(h_ref, w_ref, gn_ref, ws_ref, bs_ref, o_ref, *, tm):
    acc = _dot(h_ref[...], w_ref[...])
    act = acc * (0.5 * (1.0 + jnp.tanh(math.sqrt(2.0 / math.pi) * (acc + 0.044715 * (acc * acc * acc)))))
    u = act[:, :B_WIDTH]
    v = _rms(act[:, B_WIDTH:], gn_ref[...]).astype(BF)
    ti = lax.broadcasted_iota(jnp.int32, (SGU_CHUNK, SGU_CHUNK), 0)
    si = lax.broadcasted_iota(jnp.int32, (SGU_CHUNK, SGU_CHUNK), 1)
    for g in range(SGU_GROUPS):
        w = jnp.where(si <= ti, ws_ref[g], 0.0).astype(BF)
        cs = slice(g * SGU_GROUP_DIM, (g + 1) * SGU_GROUP_DIM)
        for c in range(tm // SGU_CHUNK):
            rs = slice(c * SGU_CHUNK, (c + 1) * SGU_CHUNK)
            mixed = _dot(w, v[rs, cs]) + bs_ref[g]
            o_ref[rs, cs] = (u[rs, cs] * mixed).astype(o_ref.dtype)


def _sgu(h, wb, layer, gn, ws, bs_b, tm):
    S, D = h.shape
    return pl.pallas_call(
        functools.partial(_sgu_kernel, tm=tm),
        grid=(S // tm,),
        in_specs=[
            pl.BlockSpec((tm, D), lambda m: (m, 0)),
            pl.BlockSpec((None, D, B_COLS), lambda m: (layer, 0, 0)),
            pl.BlockSpec((1, B_WIDTH), lambda m: (0, 0)),
            pl.BlockSpec((SGU_GROUPS, SGU_CHUNK, SGU_CHUNK), lambda m: (0, 0, 0)),
            pl.BlockSpec((SGU_GROUPS, SGU_CHUNK, SGU_GROUP_DIM), lambda m: (0, 0, 0)),
        ],
        out_specs=pl.BlockSpec((tm, B_WIDTH), lambda m: (m, 0)),
        out_shape=jax.ShapeDtypeStruct((S, B_WIDTH), BF),
        compiler_params=_params(("parallel",)),
        name="inproj_b_sgu",
    )(h, wb, gn, ws, bs_b)


def _causal_conv(ext_ref, acc, halo, w_ref, b_ref, taps, tm):
    ext_ref[0:HALO, :] = halo
    ext_ref[HALO:HALO + tm, :] = acc
    y = b_ref[...] + w_ref[taps - 1:taps, :] * acc
    for j in range(taps - 1):
        y = y + w_ref[j:j + 1, :] * ext_ref[HALO + j - taps + 1:HALO + j - taps + 1 + tm, :]
    return y


def _inproj_c_kernel(h_ref, w_ref, cw_ref, cb_ref, o_ref, ext_ref, halo_ref, *, tm, n_conv):
    m = pl.program_id(0)
    n = pl.program_id(1)
    acc = _dot(h_ref[...], w_ref[...])

    @pl.when(n < n_conv)
    def _():
        halo = jnp.where(m > 0, halo_ref[n], 0.0)
        y = _causal_conv(ext_ref, acc, halo, cw_ref, cb_ref, C_CONV, tm)
        halo_ref[n] = acc[tm - HALO:, :]
        o_ref[...] = (y * _sigmoid(y)).astype(o_ref.dtype)

    @pl.when((n >= n_conv) & (n < n_conv + n_conv // 2))
    def _():
        o_ref[...] = acc.astype(o_ref.dtype)

    @pl.when(n >= n_conv + n_conv // 2)
    def _():
        o_ref[...] = _sigmoid(acc).astype(o_ref.dtype)


def _inproj_c(h, wc, layer, cw, cb, tm):
    S, D = h.shape
    tn = 512
    n_conv = 2 * C_WIDTH // tn
    return pl.pallas_call(
        functools.partial(_inproj_c_kernel, tm=tm, n_conv=n_conv),
        grid=(S // tm, C_MAIN // tn),
        in_specs=[
            pl.BlockSpec((tm, D), lambda m, n: (m, 0)),
            pl.BlockSpec((None, D, tn), lambda m, n: (layer, 0, n)),
            pl.BlockSpec((None, C_CONV, tn), lambda m, n: (layer, 0, jnp.minimum(n, n_conv - 1))),
            pl.BlockSpec((None, 1, tn), lambda m, n: (layer, 0, jnp.minimum(n, n_conv - 1))),
        ],
        out_specs=pl.BlockSpec((tm, tn), lambda m, n: (m, n)),
        out_shape=jax.ShapeDtypeStruct((S, C_MAIN), BF),
        scratch_shapes=[pltpu.VMEM((tm + HALO, tn), F32), pltpu.VMEM((n_conv, HALO, tn), F32)],
        compiler_params=_params(("arbitrary", "arbitrary")),
        name="inproj_c",
    )(h, wc, cw, cb)


def _gates_kernel(h_ref, w_ref, b_ref, o_ref):
    o_ref[...] = _dot(h_ref[...], w_ref[...]) + b_ref[...]


def _gates(h, wgate, layer, bias, tm):
    S, D = h.shape
    return pl.pallas_call(
        _gates_kernel,
        grid=(S // tm,),
        in_specs=[
            pl.BlockSpec((tm, D), lambda m: (m, 0)),
            pl.BlockSpec((None, D, LANE), lambda m: (layer, 0, 0)),
            pl.BlockSpec((None, 1, LANE), lambda m: (layer, 0, 0)),
        ],
        out_specs=pl.BlockSpec((tm, LANE), lambda m: (m, 0)),
        out_shape=jax.ShapeDtypeStruct((S, LANE), F32),
        compiler_params=_params(("parallel",)),
        name="inproj_c_gates",
    )(h, wgate, bias)


def _lane_scan(x, op, lane):
    sh = 1
    while sh < C_CHUNK:
        x = jnp.where(lane >= sh, op(x, pltpu.roll(x, sh, x.ndim - 1)), x)
        sh *= 2
    return x


def _mlstm_prep_kernel(g_ref, r_ref, u_ref, inter_ref, eneg_ref, wk_ref, dec_ref, m_ref, bb_ref, rb_ref, *, nc):
    ig = g_ref[0:C_HEADS]
    fg = g_ref[C_HEADS:2 * C_HEADS]
    lane = lax.broadcasted_iota(jnp.int32, ig.shape, 2)
    lf = -(jnp.maximum(-fg, 0.0) + jnp.log1p(jnp.exp(-jnp.abs(fg))))
    b = _lane_scan(lf, jnp.add, lane)
    r = ig - b
    cm = _lane_scan(r, jnp.maximum, lane)
    shp = ig.shape
    bb_ref[...] = jnp.broadcast_to(b[:, :, C_CHUNK - 1:C_CHUNK], shp)
    rb_ref[...] = jnp.broadcast_to(cm[:, :, C_CHUNK - 1:C_CHUNK], shp)

    def step(c, m):
        m_ref[:, pl.ds(c, 1), :] = m
        return bb_ref[:, pl.ds(c, 1), :] + jnp.maximum(m, rb_ref[:, pl.ds(c, 1), :])

    lax.fori_loop(0, nc, step, jnp.zeros((C_HEADS, 1, C_CHUNK), F32))
    m_in = m_ref[...]
    bl = bb_ref[...]
    m_t = b + jnp.maximum(m_in, cm)
    m_new = jnp.broadcast_to(m_t[:, :, C_CHUNK - 1:C_CHUNK], shp)
    r_ref[...] = r
    u_ref[...] = b - m_t
    inter_ref[...] = jnp.exp(b + m_in - m_t)
    eneg_ref[...] = jnp.exp(-m_t)
    wk_ref[...] = jnp.exp(bl - b + ig - m_new)
    dec_ref[...] = jnp.exp(bl + m_in - m_new)


def _mlstm_prep(gate_rows, nc):
    shp = jax.ShapeDtypeStruct((C_HEADS, nc, C_CHUNK), F32)
    return pl.pallas_call(
        functools.partial(_mlstm_prep_kernel, nc=nc),
        out_shape=[shp] * 6,
        scratch_shapes=[pltpu.VMEM((C_HEADS, nc, C_CHUNK), F32)] * 3,
        compiler_params=pltpu.CompilerParams(vmem_limit_bytes=VMEM_LIMIT),
        name="mlstm_prep",
    )(gate_rows)


def _mlstm_kernel(q_ref, k_ref, v_ref, og_ref, r_ref, dec_ref, cf_ref, y_ref, c_ref, n_ref):
    c = pl.program_id(0)

    @pl.when(c == 0)
    def _():
        c_ref[...] = jnp.zeros_like(c_ref)
        n_ref[...] = jnp.zeros_like(n_ref)

    ti = lax.broadcasted_iota(jnp.int32, (C_CHUNK, C_CHUNK), 0)
    si = lax.broadcasted_iota(jnp.int32, (C_CHUNK, C_CHUNK), 1)
    causal = si <= ti
    kscale = C_HEAD_DIM ** -0.5
    cf = cf_ref[...]
    for h in range(C_HEADS):
        sl = slice(h * C_HEAD_DIM, (h + 1) * C_HEAD_DIM)
        q, k, v = q_ref[:, sl], k_ref[:, sl], v_ref[:, sl]
        u = cf[:, 4 * h:4 * h + 1]
        inter = cf[:, 4 * h + 1:4 * h + 2]
        eneg = cf[:, 4 * h + 2:4 * h + 3]
        wk = cf[:, 4 * h + 3:4 * h + 4]
        r = r_ref[h:h + 1, :]
        dec = dec_ref[h:h + 1, 0:1]
        s = lax.dot_general(q, k, (((1,), (1,)), ((), ())), preferred_element_type=F32) * kscale
        sc = s * jnp.exp(jnp.where(causal, u + r, -jnp.inf))
        cmat = c_ref[h]
        nvec = n_ref[h]
        num = _dot(sc.astype(BF), v) + inter * _dot(q, cmat.astype(BF))
        qn = jnp.sum(q.astype(F32) * nvec, axis=-1, keepdims=True)
        den = jnp.sum(sc, axis=-1, keepdims=True) + inter * qn
        hh = num / jnp.maximum(jnp.abs(den), eneg)
        y_ref[:, sl] = (og_ref[:, sl].astype(F32) * hh).astype(y_ref.dtype)
        kw = k.astype(F32) * (wk * kscale)
        upd = lax.dot_general(kw.astype(BF), v, (((0,), (0,)), ((), ())), preferred_element_type=F32)
        c_ref[h] = dec * cmat + upd
        n_ref[h] = dec * nvec + jnp.sum(kw, axis=0, keepdims=True)


def _mlstm(zc, r_rows, dec_rows, colfeat, S):
    nc = S // C_CHUNK

    def zspec(j):
        return pl.BlockSpec((C_CHUNK, C_WIDTH), lambda c: (c, j))

    return pl.pallas_call(
        _mlstm_kernel,
        grid=(nc,),
        in_specs=[
            zspec(0), zspec(1), zspec(2), zspec(3),
            pl.BlockSpec((None, C_HEADS, C_CHUNK), lambda c: (c, 0, 0)),
            pl.BlockSpec((None, C_HEADS, C_CHUNK), lambda c: (c, 0, 0)),
            pl.BlockSpec((C_CHUNK, 4 * C_HEADS), lambda c: (c, 0)),
        ],
        out_specs=pl.BlockSpec((C_CHUNK, C_WIDTH), lambda c: (c, 0)),
        out_shape=jax.ShapeDtypeStruct((S, C_WIDTH), BF),
        scratch_shapes=[pltpu.VMEM((C_HEADS, C_HEAD_DIM, C_HEAD_DIM), F32), pltpu.VMEM((C_HEADS, 1, C_HEAD_DIM), F32)],
        compiler_params=_params(("arbitrary",)),
        name="mlstm",
    )(zc, zc, zc, zc, r_rows, dec_rows, colfeat)


def _merge_kernel(h_ref, wga_ref, wgb_ref, wgc_ref, ya_ref, yb_ref, yc_ref, pa_ref, pb_ref, pc_ref, o_ref):
    h = h_ref[...]

    def branch(wg_ref, y_ref, p_ref):
        return _sigmoid(_dot(h, wg_ref[...])) * _dot(y_ref[...], p_ref[...])

    merged = branch(wga_ref, ya_ref, pa_ref) + branch(wgb_ref, yb_ref, pb_ref) + branch(wgc_ref, yc_ref, pc_ref)
    o_ref[...] = merged.astype(o_ref.dtype)


def _merge(h, wg, layer, ya, yb, yc, pa, pb, pc, tm):
    S, D = h.shape
    tn = 512
    nt = D // tn

    def gspec(b):
        return pl.BlockSpec((None, D, tn), lambda m, n: (layer, 0, b * nt + n))

    def yspec(w):
        return pl.BlockSpec((tm, w), lambda m, n: (m, 0))

    def pspec(w):
        return pl.BlockSpec((None, w, tn), lambda m, n: (layer, 0, n))

    return pl.pallas_call(
        _merge_kernel,
        grid=(S // tm, nt),
        in_specs=[pl.BlockSpec((tm, D), lambda m, n: (m, 0)), gspec(0), gspec(1), gspec(2),
                  yspec(A_WIDTH), yspec(B_WIDTH), yspec(C_WIDTH), pspec(A_WIDTH), pspec(B_WIDTH), pspec(C_WIDTH)],
        out_specs=pl.BlockSpec((tm, tn), lambda m, n: (m, n)),
        out_shape=jax.ShapeDtypeStruct((S, D), BF),
        compiler_params=_params(("parallel", "arbitrary")),
        name="merge",
    )(h, wg, wg, wg, ya, yb, yc, pa, pb, pc)


def _outproj_kernel(mg_ref, w_ref, x_ref, g_ref, xo_ref, ho_ref):
    xn = x_ref[...] + _dot(mg_ref[...], w_ref[...])
    xo_ref[...] = xn
    ho_ref[...] = _rms(xn, g_ref[...]).astype(ho_ref.dtype)


def _outproj(mg, wout, layer, x, g2, tm):
    S, D = x.shape
    return pl.pallas_call(
        _outproj_kernel,
        grid=(S // tm,),
        in_specs=[
            pl.BlockSpec((tm, D), lambda m: (m, 0)),
            pl.BlockSpec((None, D, D), lambda m: (layer, 0, 0)),
            pl.BlockSpec((tm, D), lambda m: (m, 0)),
            pl.BlockSpec((None, 1, D), lambda m: (layer, 0, 0)),
        ],
        out_specs=[pl.BlockSpec((tm, D), lambda m: (m, 0)), pl.BlockSpec((tm, D), lambda m: (m, 0))],
        out_shape=[jax.ShapeDtypeStruct((S, D), F32), jax.ShapeDtypeStruct((S, D), BF)],
        compiler_params=_params(("parallel",)),
        name="outproj",
    )(mg, wout, x, g2)


def _ffn_up_kernel(h_ref, wg_ref, wu_ref, cwg_ref, cwu_ref, cbg_ref, cbu_ref, o_ref, ext_ref, halo_ref, *, tm):
    m = pl.program_id(0)
    n = pl.program_id(1)
    h = h_ref[...]
    outs = []
    for i, (w_ref, cw_ref, cb_ref) in enumerate(((wg_ref, cwg_ref, cbg_ref), (wu_ref, cwu_ref, cbu_ref))):
        acc = _dot(h, w_ref[...])
        halo = jnp.where(m > 0, halo_ref[n, i], 0.0)
        outs.append(_causal_conv(ext_ref, acc, halo, cw_ref, cb_ref, FFN_CONV, tm))
        halo_ref[n, i] = acc[tm - HALO:, :]
    a_gate, a_up = outs
    o_ref[...] = (a_gate * _sigmoid(a_gate) * a_up).astype(o_ref.dtype)


def _ffn_up(h, wup, layer, cw, cb, tm):
    S, D = h.shape
    tn = 512
    nt = D_FF // tn

    def wspec(off):
        return pl.BlockSpec((None, D, tn), lambda m, n: (layer, 0, off + n))

    def cspec(rows, off):
        return pl.BlockSpec((None, rows, tn), lambda m, n: (layer, 0, off + n))

    return pl.pallas_call(
        functools.partial(_ffn_up_kernel, tm=tm),
        grid=(S // tm, nt),
        in_specs=[pl.BlockSpec((tm, D), lambda m, n: (m, 0)), wspec(0), wspec(nt),
                  cspec(FFN_CONV, 0), cspec(FFN_CONV, nt), cspec(1, 0), cspec(1, nt)],
        out_specs=pl.BlockSpec((tm, tn), lambda m, n: (m, n)),
        out_shape=jax.ShapeDtypeStruct((S, D_FF), BF),
        scratch_shapes=[pltpu.VMEM((tm + HALO, tn), F32), pltpu.VMEM((nt, 2, HALO, tn), F32)],
        compiler_params=_params(("arbitrary", "arbitrary")),
        name="ffn_up",
    )(h, wup, wup, cw, cw, cb, cb)


def _ffn_down_kernel(a_ref, w_ref, x_ref, g_ref, *refs, nk, last):
    acc_ref = refs[-1]
    k = pl.program_id(1)

    @pl.when(k == 0)
    def _():
        acc_ref[...] = x_ref[...]

    acc_ref[...] += _dot(a_ref[...], w_ref[...])

    @pl.when(k == nk - 1)
    def _():
        xn = acc_ref[...]
        if last:
            refs[0][...] = _rms(xn, g_ref[...])
        else:
            refs[0][...] = xn
            refs[1][...] = _rms(xn, g_ref[...]).astype(refs[1].dtype)


def _ffn_down(a, wdown, layer, x, g, last, tm):
    S, D = x.shape
    tk = 512
    nk = D_FF // tk
    row = pl.BlockSpec((tm, D), lambda m, k: (m, 0))
    if last:
        out_specs, out_shape = row, jax.ShapeDtypeStruct((S, D), F32)
    else:
        out_specs = [row, row]
        out_shape = [jax.ShapeDtypeStruct((S, D), F32), jax.ShapeDtypeStruct((S, D), BF)]
    return pl.pallas_call(
        functools.partial(_ffn_down_kernel, nk=nk, last=last),
        grid=(S // tm, nk),
        in_specs=[
            pl.BlockSpec((tm, tk), lambda m, k: (m, k)),
            pl.BlockSpec((None, tk, D), lambda m, k: (layer, k, 0)),
            row,
            pl.BlockSpec((1, D), lambda m, k: (0, 0)),
        ],
        out_specs=out_specs,
        out_shape=out_shape,
        scratch_shapes=[pltpu.VMEM((tm, D), F32)],
        compiler_params=_params(("parallel", "arbitrary")),
        name="ffn_down",
    )(a, wdown, x, g)


def _rope_lane_tables(S):
    half = ROT_DIM // 2
    inv = jnp.power(jnp.float32(ROPE_THETA), -jnp.arange(half, dtype=F32) * (2.0 / ROT_DIM))
    ang = jnp.arange(S, dtype=F32)[:, None] * inv[None, :]
    cos, sin = jnp.cos(ang), jnp.sin(ang)
    zeros = jnp.zeros((S, HEAD_DIM - ROT_DIM), F32)
    zh = jnp.zeros((S, half), F32)
    cosf = jnp.concatenate([cos, cos, jnp.ones((S, HEAD_DIM - ROT_DIM), F32)], axis=1)
    sina = jnp.concatenate([-sin, zh, zeros], axis=1)
    sinb = jnp.concatenate([zh, sin, zeros], axis=1)
    return cosf, sina, sinb


def kernel(x, norm1_g, w_in, conv_qk_w, conv_qk_b, b_igate, b_fgate, sgu_norm_g, sgu_w, sgu_b, w_branch_a, w_branch_b, w_branch_c, w_out, norm2_g, w_up, ffn_conv_w, ffn_conv_b, w_down, final_norm_g):
    B, S, D = x.shape
    depth = w_in.shape[0]
    assert B == 1 and D == D_MODEL and S % DIL_GROUPS[-1][0] == 0
    nc = S // C_CHUNK
    tm = min(1024, S)
    tm_row = min(512, S)

    o_b, o_c = A_COLS, A_COLS + B_COLS
    o_gate, o_g = o_c + C_MAIN, o_c + C_MAIN + C_GATES
    wa = w_in[:, :, :o_b].astype(BF)
    wb = w_in[:, :, o_b:o_c].astype(BF)
    wc = w_in[:, :, o_c:o_gate].astype(BF)
    wgate = jnp.pad(w_in[:, :, o_gate:o_g], ((0, 0), (0, 0), (0, LANE - C_GATES))).astype(BF)
    wg = w_in[:, :, o_g:].astype(BF)
    gate_bias = jnp.pad(jnp.concatenate([b_igate, b_fgate], axis=1), ((0, 0), (0, LANE - C_GATES)))[:, None, :].astype(F32)
    pa, pb, pc = w_branch_a.astype(BF), w_branch_b.astype(BF), w_branch_c.astype(BF)
    wout, wup, wdown = w_out.astype(BF), w_up.astype(BF), w_down.astype(BF)
    sgu_b_b = jnp.broadcast_to(sgu_b[:, :, :, None], sgu_b.shape + (SGU_GROUP_DIM,)).astype(F32)
    cosf, sina, sinb = _rope_lane_tables(S)

    xs = x[0]
    h = _rmsnorm(xs, norm1_g[0][None, :], tm_row)
    for l in range(depth):
        qkv = _inproj_a(h, wa, l, cosf, sina, sinb, tm)
        y_a = _attention(qkv, S)
        y_b = _sgu(h, wb, l, sgu_norm_g[l][None, :], sgu_w[l], sgu_b_b[l], tm_row)
        zc = _inproj_c(h, wc, l, conv_qk_w, conv_qk_b[:, None, :], tm)
        gts = _gates(h, wgate, l, gate_bias, tm)
        gate_rows = gts[:, :C_GATES].T.reshape(C_GATES, nc, C_CHUNK)
        r, u, inter, eneg, wk, dec = _mlstm_prep(gate_rows, nc)
        colfeat = jnp.stack([u, inter, eneg, wk], axis=1).transpose(2, 3, 0, 1).reshape(S, 4 * C_HEADS)
        y_c = _mlstm(zc, r.transpose(1, 0, 2), dec.transpose(1, 0, 2), colfeat, S)
        mg = _merge(h, wg, l, y_a, y_b, y_c, pa, pb, pc, tm)
        xs, h2 = _outproj(mg, wout, l, xs, norm2_g[:, None, :], tm_row)
        act = _ffn_up(h2, wup, l, ffn_conv_w, ffn_conv_b[:, None, :], tm)
        if l + 1 < depth:
            xs, h = _ffn_down(act, wdown, l, xs, norm1_g[l + 1][None, :], False, tm_row)
        else:
            xs = _ffn_down(act, wdown, l, xs, final_norm_g[None, :], True, tm_row)
    return xs[None]
```

```python
import functools
import math

import jax
import jax.numpy as jnp
from jax import lax
from jax.experimental import pallas as pl
from jax.experimental.pallas import tpu as pltpu

BF = jnp.bfloat16
F32 = jnp.float32

D_MODEL = 2048
HEAD_DIM = 128
ROT_DIM = HEAD_DIM // 4
ROPE_THETA = 500000.0
DIL_GROUPS = ((128, 1), (512, 4), (2048, 16))
N_DIL = len(DIL_GROUPS)
A_HEADS = 4
A_WIDTH = A_HEADS * HEAD_DIM
ATT_BLK = 128
ATT_TILE = 1024
SGU_CHUNK = 128
SGU_GROUPS = 4
SGU_GROUP_DIM = 128
B_WIDTH = SGU_GROUPS * SGU_GROUP_DIM
C_HEADS = 4
C_HEAD_DIM = 256
C_WIDTH = C_HEADS * C_HEAD_DIM
C_CHUNK = 128
C_CONV = 4
N_BRANCH = 3
D_FF = 5632
FFN_CONV = 3
EPS = 1e-6

A_COLS = 3 * N_DIL * A_WIDTH
B_COLS = 2 * B_WIDTH
C_MAIN = 4 * C_WIDTH
C_GATES = 2 * C_HEADS
G_COLS = N_BRANCH * D_MODEL

LANE = 128
HALO = 8
VMEM_LIMIT = 56 * 1024 * 1024


def _params(sem):
    return pltpu.CompilerParams(dimension_semantics=sem, vmem_limit_bytes=VMEM_LIMIT)


def _resident(block_shape, index_map):
    return pl.BlockSpec(block_shape, index_map, pipeline_mode=pl.Buffered(1))


def _sigmoid(x):
    return 1.0 / (1.0 + jnp.exp(-x))


def _rms(x, g):
    return x * lax.rsqrt(jnp.mean(x * x, axis=-1, keepdims=True) + EPS) * g


def _dot(a, b):
    return jnp.dot(a, b, preferred_element_type=F32)


def _att_tile(d):
    return max(ATT_TILE, ATT_BLK * d)


def _rmsnorm_kernel(x_ref, g_ref, o_ref):
    o_ref[...] = _rms(x_ref[...], g_ref[...]).astype(o_ref.dtype)


def _rmsnorm(x, g, tm):
    S, D = x.shape
    return pl.pallas_call(
        _rmsnorm_kernel,
        grid=(S // tm,),
        in_specs=[pl.BlockSpec((tm, D), lambda m: (m, 0)), pl.BlockSpec((1, D), lambda m: (0, 0))],
        out_specs=pl.BlockSpec((tm, D), lambda m: (m, 0)),
        out_shape=jax.ShapeDtypeStruct((S, D), BF),
        compiler_params=_params(("parallel",)),
        name="rmsnorm",
    )(x, g)


def _inproj_a_kernel(h_ref, wq_ref, wk_ref, wv_ref, cos_ref, sa_ref, sb_ref, o_ref, *scratch, d, tm):
    h = h_ref[...]
    c, sa, sb = cos_ref[...], sa_ref[...], sb_ref[...]
    for t, w_ref in enumerate((wq_ref, wk_ref, wv_ref)):
        acc = _dot(h, w_ref[...])
        for j in range(A_HEADS):
            xj = acc[:, j * HEAD_DIM:(j + 1) * HEAD_DIM]
            if t < 2:
                xj = xj * c + pltpu.roll(xj, HEAD_DIM - ROT_DIM // 2, 1) * sa + pltpu.roll(xj, ROT_DIM // 2, 1) * sb
            if t == 0:
                xj = xj * HEAD_DIM ** -0.5
            cols = slice(t * A_WIDTH + j * HEAD_DIM, t * A_WIDTH + (j + 1) * HEAD_DIM)
            if d == 1:
                o_ref[0, :, cols] = xj.astype(o_ref.dtype)
            else:
                scr = scratch[0]
                scr[t, j] = xj
                for r in range(d):
                    o_ref[r, :, cols] = scr[t, j, pl.ds(r, tm // d, stride=d), :].astype(o_ref.dtype)


def _inproj_a(h, wa, layer, g, cosf, sina, sinb, tm):
    S, D = h.shape
    _, d = DIL_GROUPS[g]
    T = _att_tile(d)
    per = T // tm

    def wspec(t):
        return pl.BlockSpec((None, D, A_WIDTH), lambda m: (layer, 0, t * N_DIL + g))

    tab = pl.BlockSpec((tm, HEAD_DIM), lambda m: (m, 0))
    return pl.pallas_call(
        functools.partial(_inproj_a_kernel, d=d, tm=tm),
        grid=(S // tm,),
        in_specs=[pl.BlockSpec((tm, D), lambda m: (m, 0)), wspec(0), wspec(1), wspec(2), tab, tab, tab],
        out_specs=pl.BlockSpec((None, d, tm // d, 3 * A_WIDTH), lambda m: (m // per, 0, m % per, 0)),
        out_shape=jax.ShapeDtypeStruct((S // T, d, T // d, 3 * A_WIDTH), BF),
        scratch_shapes=[] if d == 1 else [pltpu.VMEM((3, A_HEADS, tm, HEAD_DIM), F32)],
        compiler_params=_params(("parallel",)),
        name=f"inproj_a_g{g}",
    )(h, wa, wa, wa, cosf, sina, sinb)


def _attn_kernel(*refs, d, tile, chain_in, final):
    qc_ref, kc_ref, vc_ref, kp_ref, vp_ref = refs[:5]
    pos = 5
    if chain_in:
        oin_ref, lin_ref = refs[pos:pos + 2]
        pos += 2
    if final:
        y_ref = refs[pos]
        pos += 1
    else:
        o_ref, l_ref = refs[pos:pos + 2]
        pos += 2
    kfull, vfull = refs[pos:pos + 2]

    n = pl.program_id(0)
    kfull[:, 0:ATT_BLK, :] = kp_ref[...]
    kfull[:, ATT_BLK:, :] = kc_ref[...]
    vfull[:, 0:ATT_BLK, :] = vp_ref[...]
    vfull[:, ATT_BLK:, :] = vc_ref[...]
    i = lax.broadcasted_iota(jnp.int32, (ATT_BLK, 2 * ATT_BLK), 0)
    kk = lax.broadcasted_iota(jnp.int32, (ATT_BLK, 2 * ATT_BLK), 1)
    band = (kk >= i) & (kk <= i + ATT_BLK)

    def pair(p, carry):
        r = p % d
        b = p // d
        lim = jnp.where((n == 0) & (b == 0), ATT_BLK, 0)
        mask = band & (kk >= lim)
        row0 = pl.multiple_of(b * ATT_BLK, ATT_BLK)
        start = b * (ATT_BLK * d) + r
        for hd in range(A_HEADS):
            sl = slice(hd * HEAD_DIM, (hd + 1) * HEAD_DIM)
            q = qc_ref[r, pl.ds(row0, ATT_BLK), sl]
            k = kfull[r, pl.ds(row0, 2 * ATT_BLK), sl]
            v = vfull[r, pl.ds(row0, 2 * ATT_BLK), sl]
            s = lax.dot_general(q, k, (((1,), (1,)), ((), ())), preferred_element_type=F32)
            s = jnp.where(mask, s, -jnp.inf)
            m = jnp.max(s, axis=-1, keepdims=True)
            e = jnp.exp(s - m)
            l = jnp.sum(e, axis=-1, keepdims=True)
            o = _dot(e.astype(BF), v) / l
            lse = jnp.broadcast_to(m + jnp.log(l), (ATT_BLK, HEAD_DIM))
            rows = pl.ds(start, ATT_BLK, stride=d) if d > 1 else pl.ds(pl.multiple_of(start, ATT_BLK), ATT_BLK)
            if chain_in:
                o2 = oin_ref[hd, rows, :]
                l2 = lin_ref[hd, rows, :]
                mx = jnp.maximum(lse, l2)
                w1, w2 = jnp.exp(lse - mx), jnp.exp(l2 - mx)
                den = w1 + w2
                o = (w1 * o + w2 * o2) / den
                lse = mx + jnp.log(den)
            if final:
                y_ref[rows, sl] = o.astype(y_ref.dtype)
            else:
                o_ref[hd, rows, :] = o
                l_ref[hd, rows, :] = lse
        return carry

    lax.fori_loop(0, tile // ATT_BLK, pair, 0, unroll=2)


def _attn_group(a_g, g, S, chain):
    _, d = DIL_GROUPS[g]
    T = _att_tile(d)
    J = T // d
    final = g == 0
    assert not (final and d != 1)

    def cur(c):
        return pl.BlockSpec((None, d, J, A_WIDTH), lambda n: (n, 0, 0, c))

    def prev(c):
        return pl.BlockSpec((None, d, ATT_BLK, A_WIDTH), lambda n: (jnp.maximum(n - 1, 0), 0, J // ATT_BLK - 1, c))

    hm = pl.BlockSpec((A_HEADS, T, HEAD_DIM), lambda n: (0, n, 0))
    in_specs = [cur(0), cur(1), cur(2), prev(1), prev(2)]
    args = [a_g] * 5
    if chain is not None:
        in_specs += [hm, hm]
        args += list(chain)
    if final:
        out_specs = pl.BlockSpec((T, A_WIDTH), lambda n: (n, 0))
        out_shape = jax.ShapeDtypeStruct((S, A_WIDTH), BF)
    else:
        out_specs = [hm, hm]
        out_shape = [jax.ShapeDtypeStruct((A_HEADS, S, HEAD_DIM), F32)] * 2
    return pl.pallas_call(
        functools.partial(_attn_kernel, d=d, tile=T, chain_in=chain is not None, final=final),
        grid=(S // T,),
        in_specs=in_specs,
        out_specs=out_specs,
        out_shape=out_shape,
        scratch_shapes=[pltpu.VMEM((d, ATT_BLK + J, A_WIDTH), BF)] * 2,
        compiler_params=_params(("arbitrary",)),
        name=f"attn_g{g}",
    )(*args)


def _sgu_kernel(h_ref, w_ref, gn_ref, ws_ref, bs_ref, o_ref, *, tm):
    acc = _dot(h_ref[...], w_ref[...])
    act = acc * (0.5 * (1.0 + jnp.tanh(math.sqrt(2.0 / math.pi) * (acc + 0.044715 * (acc * acc * acc)))))
    u = act[:, :B_WIDTH]
    v = _rms(act[:, B_WIDTH:], gn_ref[...]).astype(BF)
    ti = lax.broadcasted_iota(jnp.int32, (SGU_CHUNK, SGU_CHUNK), 0)
    si = lax.broadcasted_iota(jnp.int32, (SGU_CHUNK, SGU_CHUNK), 1)
    for g in range(SGU_GROUPS):
        w = jnp.where(si <= ti, ws_ref[g], 0.0).astype(BF)
        cs = slice(g * SGU_GROUP_DIM, (g + 1) * SGU_GROUP_DIM)
        for c in range(tm // SGU_CHUNK):
            rs = slice(c * SGU_CHUNK, (c + 1) * SGU_CHUNK)
            mixed = _dot(w, v[rs, cs]) + bs_ref[g]
            o_ref[rs, cs] = (u[rs, cs] * mixed).astype(o_ref.dtype)


def _sgu(h, wb, layer, gn, ws, bs_b, tm):
    S, D = h.shape
    return pl.pallas_call(
        functools.partial(_sgu_kernel, tm=tm),
        grid=(S // tm,),
        in_specs=[
            pl.BlockSpec((tm, D), lambda m: (m, 0)),
            pl.BlockSpec((None, D, B_COLS), lambda m: (layer, 0, 0)),
            pl.BlockSpec((1, B_WIDTH), lambda m: (0, 0)),
            pl.BlockSpec((SGU_GROUPS, SGU_CHUNK, SGU_CHUNK), lambda m: (0, 0, 0)),
            pl.BlockSpec((SGU_GROUPS, SGU_CHUNK, SGU_GROUP_DIM), lambda m: (0, 0, 0)),
        ],
        out_specs=pl.BlockSpec((tm, B_WIDTH), lambda m: (m, 0)),
        out_shape=jax.ShapeDtypeStruct((S, B_WIDTH), BF),
        compiler_params=_params(("parallel",)),
        name="inproj_b_sgu",
    )(h, wb, gn, ws, bs_b)


def _causal_conv(ext_ref, acc, halo, w, b, taps, tm):
    ext_ref[0:HALO, :] = halo
    ext_ref[HALO:HALO + tm, :] = acc
    y = b + w[taps - 1:taps, :] * acc
    for j in range(taps - 1):
        y = y + w[j:j + 1, :] * ext_ref[HALO + j - taps + 1:HALO + j - taps + 1 + tm, :]
    return y


def _inproj_c_kernel(h_ref, wa_ref, wb_ref, wv_ref, wo_ref, cw_ref, cb_ref, qk_ref, v_ref, og_ref, ext_ref, halo_ref, *, tm, tn):
    m = pl.program_id(0)
    j = pl.program_id(1)
    h = h_ref[...]
    for i, w_ref in enumerate((wa_ref, wb_ref)):
        cs = slice(i * tn, (i + 1) * tn)
        acc = _dot(h, w_ref[...])
        halo = jnp.where(m > 0, halo_ref[j, i], 0.0)
        y = _causal_conv(ext_ref.at[i], acc, halo, cw_ref[:, cs], cb_ref[:, cs], C_CONV, tm)
        halo_ref[j, i] = acc[tm - HALO:, :]
        qk_ref[:, cs] = (y * _sigmoid(y)).astype(qk_ref.dtype)
    v_ref[...] = _dot(h, wv_ref[...]).astype(v_ref.dtype)
    og_ref[...] = _sigmoid(_dot(h, wo_ref[...])).astype(og_ref.dtype)


def _inproj_c(h, wc, layer, cw, cb, tm):
    S, D = h.shape
    tn = 512
    nj = C_WIDTH // tn
    qk_slabs = 2 * C_WIDTH // tn

    def wspec(fn):
        return pl.BlockSpec((None, D, tn), lambda m, j: (layer, 0, fn(j)))

    return pl.pallas_call(
        functools.partial(_inproj_c_kernel, tm=tm, tn=tn),
        grid=(S // tm, nj),
        in_specs=[
            pl.BlockSpec((tm, D), lambda m, j: (m, 0)),
            wspec(lambda j: 2 * j), wspec(lambda j: 2 * j + 1),
            wspec(lambda j: qk_slabs + j), wspec(lambda j: qk_slabs + nj + j),
            pl.BlockSpec((None, C_CONV, 2 * tn), lambda m, j: (layer, 0, j)),
            pl.BlockSpec((None, 1, 2 * tn), lambda m, j: (layer, 0, j)),
        ],
        out_specs=[pl.BlockSpec((tm, 2 * tn), lambda m, j: (m, j)),
                   pl.BlockSpec((tm, tn), lambda m, j: (m, j)),
                   pl.BlockSpec((tm, tn), lambda m, j: (m, j))],
        out_shape=[jax.ShapeDtypeStruct((S, 2 * C_WIDTH), BF),
                   jax.ShapeDtypeStruct((S, C_WIDTH), BF),
                   jax.ShapeDtypeStruct((S, C_WIDTH), BF)],
        scratch_shapes=[pltpu.VMEM((2, tm + HALO, tn), F32), pltpu.VMEM((nj, 2, HALO, tn), F32)],
        compiler_params=_params(("arbitrary", "arbitrary")),
        name="inproj_c",
    )(h, wc, wc, wc, wc, cw, cb)


def _gates_kernel(h_ref, w_ref, b_ref, o_ref):
    o_ref[...] = _dot(h_ref[...], w_ref[...]) + b_ref[...]


def _gates(h, wgate, layer, bias, tm):
    S, D = h.shape
    return pl.pallas_call(
        _gates_kernel,
        grid=(S // tm,),
        in_specs=[
            pl.BlockSpec((tm, D), lambda m: (m, 0)),
            pl.BlockSpec((None, D, LANE), lambda m: (layer, 0, 0)),
            pl.BlockSpec((None, 1, LANE), lambda m: (layer, 0, 0)),
        ],
        out_specs=pl.BlockSpec((tm, LANE), lambda m: (m, 0)),
        out_shape=jax.ShapeDtypeStruct((S, LANE), F32),
        compiler_params=_params(("parallel",)),
        name="inproj_c_gates",
    )(h, wgate, bias)


def _lane_scan(x, op, lane):
    sh = 1
    while sh < C_CHUNK:
        x = jnp.where(lane >= sh, op(x, pltpu.roll(x, sh, x.ndim - 1)), x)
        sh *= 2
    return x


def _mlstm_prep_kernel(g_ref, r_ref, u_ref, inter_ref, eneg_ref, wk_ref, dec_ref, m_ref, bb_ref, rb_ref, *, nc):
    ig = g_ref[0:C_HEADS]
    fg = g_ref[C_HEADS:2 * C_HEADS]
    lane = lax.broadcasted_iota(jnp.int32, ig.shape, 2)
    lf = -(jnp.maximum(-fg, 0.0) + jnp.log1p(jnp.exp(-jnp.abs(fg))))
    b = _lane_scan(lf, jnp.add, lane)
    r = ig - b
    cm = _lane_scan(r, jnp.maximum, lane)
    shp = ig.shape
    bb_ref[...] = jnp.broadcast_to(b[:, :, C_CHUNK - 1:C_CHUNK], shp)
    rb_ref[...] = jnp.broadcast_to(cm[:, :, C_CHUNK - 1:C_CHUNK], shp)

    def step(c, m):
        m_ref[:, pl.ds(c, 1), :] = m
        return bb_ref[:, pl.ds(c, 1), :] + jnp.maximum(m, rb_ref[:, pl.ds(c, 1), :])

    lax.fori_loop(0, nc, step, jnp.zeros((C_HEADS, 1, C_CHUNK), F32))
    m_in = m_ref[...]
    bl = bb_ref[...]
    m_t = b + jnp.maximum(m_in, cm)
    m_new = jnp.broadcast_to(m_t[:, :, C_CHUNK - 1:C_CHUNK], shp)
    r_ref[...] = r
    u_ref[...] = b - m_t
    inter_ref[...] = jnp.exp(b + m_in - m_t)
    eneg_ref[...] = jnp.exp(-m_t)
    wk_ref[...] = jnp.exp(bl - b + ig - m_new)
    dec_ref[...] = jnp.exp(bl + m_in - m_new)


def _mlstm_prep(gate_rows, nc):
    shp = jax.ShapeDtypeStruct((C_HEADS, nc, C_CHUNK), F32)
    return pl.pallas_call(
        functools.partial(_mlstm_prep_kernel, nc=nc),
        out_shape=[shp] * 6,
        scratch_shapes=[pltpu.VMEM((C_HEADS, nc, C_CHUNK), F32)] * 3,
        compiler_params=pltpu.CompilerParams(vmem_limit_bytes=VMEM_LIMIT),
        name="mlstm_prep",
    )(gate_rows)


def _mlstm_kernel(q_ref, k_ref, v_ref, og_ref, r_ref, dec_ref, cf_ref, y_ref, c_ref, n_ref):
    c = pl.program_id(0)

    @pl.when(c == 0)
    def _():
        c_ref[...] = jnp.zeros_like(c_ref)
        n_ref[...] = jnp.zeros_like(n_ref)

    ti = lax.broadcasted_iota(jnp.int32, (C_CHUNK, C_CHUNK), 0)
    si = lax.broadcasted_iota(jnp.int32, (C_CHUNK, C_CHUNK), 1)
    causal = si <= ti
    kscale = C_HEAD_DIM ** -0.5
    cf = cf_ref[...]
    for h in range(C_HEADS):
        sl = slice(h * C_HEAD_DIM, (h + 1) * C_HEAD_DIM)
        q, k, v = q_ref[:, sl], k_ref[:, sl], v_ref[:, sl]
        u = cf[:, 4 * h:4 * h + 1]
        inter = cf[:, 4 * h + 1:4 * h + 2]
        eneg = cf[:, 4 * h + 2:4 * h + 3]
        wk = cf[:, 4 * h + 3:4 * h + 4]
        r = r_ref[h:h + 1, :]
        dec = dec_ref[h:h + 1, 0:1]
        s = lax.dot_general(q, k, (((1,), (1,)), ((), ())), preferred_element_type=F32) * kscale
        sc = s * jnp.exp(jnp.where(causal, u + r, -jnp.inf))
        cmat = c_ref[h]
        nvec = n_ref[h]
        num = _dot(sc.astype(BF), v) + inter * _dot(q, cmat.astype(BF))
        qn = jnp.sum(q.astype(F32) * nvec, axis=-1, keepdims=True)
        den = jnp.sum(sc, axis=-1, keepdims=True) + inter * qn
        hh = num / jnp.maximum(jnp.abs(den), eneg)
        y_ref[:, sl] = (og_ref[:, sl].astype(F32) * hh).astype(y_ref.dtype)
        kw = k.astype(F32) * (wk * kscale)
        upd = lax.dot_general(kw.astype(BF), v, (((0,), (0,)), ((), ())), preferred_element_type=F32)
        c_ref[h] = dec * cmat + upd
        n_ref[h] = dec * nvec + jnp.sum(kw, axis=0, keepdims=True)


def _mlstm(qk, v, og, r_rows, dec_rows, colfeat, S):
    nc = S // C_CHUNK
    row = pl.BlockSpec((C_CHUNK, C_WIDTH), lambda c: (c, 0))
    return pl.pallas_call(
        _mlstm_kernel,
        grid=(nc,),
        in_specs=[
            row, pl.BlockSpec((C_CHUNK, C_WIDTH), lambda c: (c, 1)), row, row,
            pl.BlockSpec((None, C_HEADS, C_CHUNK), lambda c: (c, 0, 0)),
            pl.BlockSpec((None, C_HEADS, C_CHUNK), lambda c: (c, 0, 0)),
            pl.BlockSpec((C_CHUNK, 4 * C_HEADS), lambda c: (c, 0)),
        ],
        out_specs=row,
        out_shape=jax.ShapeDtypeStruct((S, C_WIDTH), BF),
        scratch_shapes=[pltpu.VMEM((C_HEADS, C_HEAD_DIM, C_HEAD_DIM), F32), pltpu.VMEM((C_HEADS, 1, C_HEAD_DIM), F32)],
        compiler_params=_params(("arbitrary",)),
        name="mlstm",
    )(qk, qk, v, og, r_rows, dec_rows, colfeat)


def _merge_kernel(h_ref, wga_ref, wgb_ref, wgc_ref, ya_ref, yb_ref, yc_ref, pa_ref, pb_ref, pc_ref, o_ref):
    h = h_ref[...]

    def branch(wg_ref, y_ref, p_ref):
        return _sigmoid(_dot(h, wg_ref[...])) * _dot(y_ref[...], p_ref[...])

    merged = branch(wga_ref, ya_ref, pa_ref) + branch(wgb_ref, yb_ref, pb_ref) + branch(wgc_ref, yc_ref, pc_ref)
    o_ref[...] = merged.astype(o_ref.dtype)


def _merge(h, wg, layer, ya, yb, yc, pa, pb, pc, tm):
    S, D = h.shape
    tn = 512
    nt = D // tn

    def gspec(b):
        return pl.BlockSpec((None, D, tn), lambda m, n: (layer, 0, b * nt + n))

    def yspec(w):
        return pl.BlockSpec((tm, w), lambda m, n: (m, 0))

    def pspec(w):
        return pl.BlockSpec((None, w, tn), lambda m, n: (layer, 0, n))

    return pl.pallas_call(
        _merge_kernel,
        grid=(S // tm, nt),
        in_specs=[pl.BlockSpec((tm, D), lambda m, n: (m, 0)), gspec(0), gspec(1), gspec(2),
                  yspec(A_WIDTH), yspec(B_WIDTH), yspec(C_WIDTH), pspec(A_WIDTH), pspec(B_WIDTH), pspec(C_WIDTH)],
        out_specs=pl.BlockSpec((tm, tn), lambda m, n: (m, n)),
        out_shape=jax.ShapeDtypeStruct((S, D), BF),
        compiler_params=_params(("parallel", "arbitrary")),
        name="merge",
    )(h, wg, wg, wg, ya, yb, yc, pa, pb, pc)


def _outproj_kernel(mg_ref, w_ref, x_ref, g_ref, xo_ref, ho_ref):
    xn = x_ref[...] + _dot(mg_ref[...], w_ref[...])
    xo_ref[...] = xn
    ho_ref[...] = _rms(xn, g_ref[...]).astype(ho_ref.dtype)


def _outproj(mg, wout, layer, x, g2, tm):
    S, D = x.shape
    return pl.pallas_call(
        _outproj_kernel,
        grid=(S // tm,),
        in_specs=[
            pl.BlockSpec((tm, D), lambda m: (m, 0)),
            _resident((None, D, D), lambda m: (layer, 0, 0)),
            pl.BlockSpec((tm, D), lambda m: (m, 0)),
            pl.BlockSpec((None, 1, D), lambda m: (layer, 0, 0)),
        ],
        out_specs=[pl.BlockSpec((tm, D), lambda m: (m, 0)), pl.BlockSpec((tm, D), lambda m: (m, 0))],
        out_shape=[jax.ShapeDtypeStruct((S, D), F32), jax.ShapeDtypeStruct((S, D), BF)],
        compiler_params=_params(("parallel",)),
        name="outproj",
    )(mg, wout, x, g2)


def _ffn_up_kernel(h_ref, wg_ref, wu_ref, cwg_ref, cwu_ref, cbg_ref, cbu_ref, o_ref, ext_ref, halo_ref, *, tm):
    m = pl.program_id(0)
    n = pl.program_id(1)
    h = h_ref[...]
    outs = []
    for i, (w_ref, cw_ref, cb_ref) in enumerate(((wg_ref, cwg_ref, cbg_ref), (wu_ref, cwu_ref, cbu_ref))):
        acc = _dot(h, w_ref[...])
        halo = jnp.where(m > 0, halo_ref[n, i], 0.0)
        outs.append(_causal_conv(ext_ref.at[i], acc, halo, cw_ref[...], cb_ref[...], FFN_CONV, tm))
        halo_ref[n, i] = acc[tm - HALO:, :]
    a_gate, a_up = outs
    o_ref[...] = (a_gate * _sigmoid(a_gate) * a_up).astype(o_ref.dtype)


def _ffn_up(h, wup, layer, cw, cb, tm):
    S, D = h.shape
    tn = 512
    nt = D_FF // tn

    def wspec(off):
        return pl.BlockSpec((None, D, tn), lambda m, n: (layer, 0, off + n))

    def cspec(rows, off):
        return pl.BlockSpec((None, rows, tn), lambda m, n: (layer, 0, off + n))

    return pl.pallas_call(
        functools.partial(_ffn_up_kernel, tm=tm),
        grid=(S // tm, nt),
        in_specs=[pl.BlockSpec((tm, D), lambda m, n: (m, 0)), wspec(0), wspec(nt),
                  cspec(FFN_CONV, 0), cspec(FFN_CONV, nt), cspec(1, 0), cspec(1, nt)],
        out_specs=pl.BlockSpec((tm, tn), lambda m, n: (m, n)),
        out_shape=jax.ShapeDtypeStruct((S, D_FF), BF),
        scratch_shapes=[pltpu.VMEM((2, tm + HALO, tn), F32), pltpu.VMEM((nt, 2, HALO, tn), F32)],
        compiler_params=_params(("arbitrary", "arbitrary")),
        name="ffn_up",
    )(h, wup, wup, cw, cw, cb, cb)


def _ffn_down_kernel(a_ref, w_ref, x_ref, g_ref, *out_refs, last):
    xn = x_ref[...] + _dot(a_ref[...], w_ref[...])
    if last:
        out_refs[0][...] = _rms(xn, g_ref[...])
    else:
        out_refs[0][...] = xn
        out_refs[1][...] = _rms(xn, g_ref[...]).astype(out_refs[1].dtype)


def _ffn_down(a, wdown, layer, x, g, last, tm):
    S, D = x.shape
    row = pl.BlockSpec((tm, D), lambda m: (m, 0))
    if last:
        out_specs, out_shape = row, jax.ShapeDtypeStruct((S, D), F32)
    else:
        out_specs = [row, row]
        out_shape = [jax.ShapeDtypeStruct((S, D), F32), jax.ShapeDtypeStruct((S, D), BF)]
    return pl.pallas_call(
        functools.partial(_ffn_down_kernel, last=last),
        grid=(S // tm,),
        in_specs=[
            pl.BlockSpec((tm, D_FF), lambda m: (m, 0)),
            _resident((None, D_FF, D), lambda m: (layer, 0, 0)),
            row,
            pl.BlockSpec((1, D), lambda m: (0, 0)),
        ],
        out_specs=out_specs,
        out_shape=out_shape,
        compiler_params=_params(("parallel",)),
        name="ffn_down",
    )(a, wdown, x, g)


def _rope_lane_tables(S):
    half = ROT_DIM // 2
    inv = jnp.power(jnp.float32(ROPE_THETA), -jnp.arange(half, dtype=F32) * (2.0 / ROT_DIM))
    ang = jnp.arange(S, dtype=F32)[:, None] * inv[None, :]
    cos, sin = jnp.cos(ang), jnp.sin(ang)
    zeros = jnp.zeros((S, HEAD_DIM - ROT_DIM), F32)
    zh = jnp.zeros((S, half), F32)
    cosf = jnp.concatenate([cos, cos, jnp.ones((S, HEAD_DIM - ROT_DIM), F32)], axis=1)
    sina = jnp.concatenate([-sin, zh, zeros], axis=1)
    sinb = jnp.concatenate([zh, sin, zeros], axis=1)
    return cosf, sina, sinb


def kernel(x, norm1_g, w_in, conv_qk_w, conv_qk_b, b_igate, b_fgate, sgu_norm_g, sgu_w, sgu_b, w_branch_a, w_branch_b, w_branch_c, w_out, norm2_g, w_up, ffn_conv_w, ffn_conv_b, w_down, final_norm_g):
    B, S, D = x.shape
    depth = w_in.shape[0]
    assert B == 1 and D == D_MODEL and S % DIL_GROUPS[-1][0] == 0
    nc = S // C_CHUNK
    tm = min(1024, S)
    tm_row = min(512, S)
    tm_down = min(256, S)

    o_b, o_c = A_COLS, A_COLS + B_COLS
    o_gate, o_g = o_c + C_MAIN, o_c + C_MAIN + C_GATES
    wa = w_in[:, :, :o_b].astype(BF)
    wb = w_in[:, :, o_b:o_c].astype(BF)
    wc = w_in[:, :, o_c:o_gate].astype(BF)
    wgate = jnp.pad(w_in[:, :, o_gate:o_g], ((0, 0), (0, 0), (0, LANE - C_GATES))).astype(BF)
    wg = w_in[:, :, o_g:].astype(BF)
    gate_bias = jnp.pad(jnp.concatenate([b_igate, b_fgate], axis=1), ((0, 0), (0, LANE - C_GATES)))[:, None, :].astype(F32)
    pa, pb, pc = w_branch_a.astype(BF), w_branch_b.astype(BF), w_branch_c.astype(BF)
    wout, wup, wdown = w_out.astype(BF), w_up.astype(BF), w_down.astype(BF)
    sgu_b_b = jnp.broadcast_to(sgu_b[:, :, :, None], sgu_b.shape + (SGU_GROUP_DIM,)).astype(F32)
    cosf, sina, sinb = _rope_lane_tables(S)

    xs = x[0]
    h = _rmsnorm(xs, norm1_g[0][None, :], tm_row)
    for l in range(depth):
        chain = None
        for g in reversed(range(N_DIL)):
            a_g = _inproj_a(h, wa, l, g, cosf, sina, sinb, tm)
            chain = _attn_group(a_g, g, S, chain)
        y_a = chain
        y_b = _sgu(h, wb, l, sgu_norm_g[l][None, :], sgu_w[l], sgu_b_b[l], tm_row)
        qk_c, v_c, og_c = _inproj_c(h, wc, l, conv_qk_w, conv_qk_b[:, None, :], tm)
        gts = _gates(h, wgate, l, gate_bias, tm)
        gate_rows = gts[:, :C_GATES].T.reshape(C_GATES, nc, C_CHUNK)
        r, u, inter, eneg, wk, dec = _mlstm_prep(gate_rows, nc)
        colfeat = jnp.stack([u, inter, eneg, wk], axis=1).transpose(2, 3, 0, 1).reshape(S, 4 * C_HEADS)
        y_c = _mlstm(qk_c, v_c, og_c, r.transpose(1, 0, 2), dec.transpose(1, 0, 2), colfeat, S)
        mg = _merge(h, wg, l, y_a, y_b, y_c, pa, pb, pc, tm)
        xs, h2 = _outproj(mg, wout, l, xs, norm2_g[:, None, :], tm_row)
        act = _ffn_up(h2, wup, l, ffn_conv_w, ffn_conv_b[:, None, :], tm)
        if l + 1 < depth:
            xs, h = _ffn_down(act, wdown, l, xs, norm1_g[l + 1][None, :], False, tm_down)
        else:
            xs = _ffn_down(act, wdown, l, xs, final_norm_g[None, :], True, tm_down)
    return xs[None]
```

```python
import functools
import math

import jax
import jax.numpy as jnp
from jax import lax
from jax.experimental import pallas as pl
from jax.experimental.pallas import tpu as pltpu

BF = jnp.bfloat16
F32 = jnp.float32

D_MODEL = 2048
HEAD_DIM = 128
ROT_DIM = HEAD_DIM // 4
ROPE_THETA = 500000.0
DIL_GROUPS = ((128, 1), (512, 4), (2048, 16))
N_DIL = len(DIL_GROUPS)
A_HEADS = 4
A_WIDTH = A_HEADS * HEAD_DIM
ATT_BLK = 128
ATT_TILE = 1024
SGU_CHUNK = 128
SGU_GROUPS = 4
SGU_GROUP_DIM = 128
B_WIDTH = SGU_GROUPS * SGU_GROUP_DIM
C_HEADS = 4
C_HEAD_DIM = 256
C_WIDTH = C_HEADS * C_HEAD_DIM
C_CHUNK = 128
C_CONV = 4
N_BRANCH = 3
D_FF = 5632
FFN_CONV = 3
EPS = 1e-6

A_COLS = 3 * N_DIL * A_WIDTH
B_COLS = 2 * B_WIDTH
C_MAIN = 4 * C_WIDTH
C_GATES = 2 * C_HEADS
G_COLS = N_BRANCH * D_MODEL

LANE = 128
LOG2E = math.log2(math.e)
HALO = 8
VMEM_LIMIT = 56 * 1024 * 1024


def _params(sem):
    return pltpu.CompilerParams(dimension_semantics=sem, vmem_limit_bytes=VMEM_LIMIT)


def _resident(block_shape, index_map):
    return pl.BlockSpec(block_shape, index_map, pipeline_mode=pl.Buffered(1))


def _sigmoid(x):
    return 1.0 / (1.0 + jnp.exp(-x))


def _rms(x, g):
    return x * lax.rsqrt(jnp.mean(x * x, axis=-1, keepdims=True) + EPS) * g


def _dot(a, b):
    return jnp.dot(a, b, preferred_element_type=F32)


def _att_tile(d):
    return max(ATT_TILE, ATT_BLK * d)


def _rmsnorm_kernel(x_ref, g_ref, o_ref):
    o_ref[...] = _rms(x_ref[...], g_ref[...]).astype(o_ref.dtype)


def _rmsnorm(x, g, tm):
    S, D = x.shape
    return pl.pallas_call(
        _rmsnorm_kernel,
        grid=(S // tm,),
        in_specs=[pl.BlockSpec((tm, D), lambda m: (m, 0)), pl.BlockSpec((1, D), lambda m: (0, 0))],
        out_specs=pl.BlockSpec((tm, D), lambda m: (m, 0)),
        out_shape=jax.ShapeDtypeStruct((S, D), BF),
        compiler_params=_params(("parallel",)),
        name="rmsnorm",
    )(x, g)


def _inproj_a_kernel(h_ref, wq_ref, wk_ref, wv_ref, cos_ref, sa_ref, sb_ref, o_ref, *scratch, d, tm):
    h = h_ref[...]
    c, sa, sb = cos_ref[...], sa_ref[...], sb_ref[...]
    for t, w_ref in enumerate((wq_ref, wk_ref, wv_ref)):
        acc = _dot(h, w_ref[...])
        for j in range(A_HEADS):
            xj = acc[:, j * HEAD_DIM:(j + 1) * HEAD_DIM]
            if t < 2:
                xj = xj * c + pltpu.roll(xj, HEAD_DIM - ROT_DIM // 2, 1) * sa + pltpu.roll(xj, ROT_DIM // 2, 1) * sb
            if t == 0:
                xj = xj * (HEAD_DIM ** -0.5 * LOG2E)
            cols = slice(t * A_WIDTH + j * HEAD_DIM, t * A_WIDTH + (j + 1) * HEAD_DIM)
            if d == 1:
                o_ref[0, :, cols] = xj.astype(o_ref.dtype)
            else:
                scr = scratch[0]
                scr[t, j] = xj
                for r in range(d):
                    o_ref[r, :, cols] = scr[t, j, pl.ds(r, tm // d, stride=d), :].astype(o_ref.dtype)


def _inproj_a(h, wa, layer, g, cosf, sina, sinb, tm):
    S, D = h.shape
    _, d = DIL_GROUPS[g]
    T = _att_tile(d)
    per = T // tm

    def wspec(t):
        return pl.BlockSpec((None, D, A_WIDTH), lambda m: (layer, 0, t * N_DIL + g))

    tab = pl.BlockSpec((tm, HEAD_DIM), lambda m: (m, 0))
    return pl.pallas_call(
        functools.partial(_inproj_a_kernel, d=d, tm=tm),
        grid=(S // tm,),
        in_specs=[pl.BlockSpec((tm, D), lambda m: (m, 0)), wspec(0), wspec(1), wspec(2), tab, tab, tab],
        out_specs=pl.BlockSpec((None, d, tm // d, 3 * A_WIDTH), lambda m: (m // per, 0, m % per, 0)),
        out_shape=jax.ShapeDtypeStruct((S // T, d, T // d, 3 * A_WIDTH), BF),
        scratch_shapes=[] if d == 1 else [pltpu.VMEM((3, A_HEADS, tm, HEAD_DIM), F32)],
        compiler_params=_params(("parallel",)),
        name=f"inproj_a_g{g}",
    )(h, wa, wa, wa, cosf, sina, sinb)


def _attn_kernel(*refs, d, tile, chain_in, final):
    qc_ref, kc_ref, vc_ref, kp_ref, vp_ref = refs[:5]
    pos = 5
    if chain_in:
        oin_ref, lin_ref = refs[pos:pos + 2]
        pos += 2
    if final:
        y_ref = refs[pos]
        pos += 1
    else:
        o_ref, l_ref = refs[pos:pos + 2]
        pos += 2
    kfull, vfull, bias_ref = refs[pos:pos + 3]

    n = pl.program_id(0)
    kfull[:, 0:ATT_BLK, :] = kp_ref[...]
    kfull[:, ATT_BLK:, :] = kc_ref[...]
    vfull[:, 0:ATT_BLK, :] = vp_ref[...]
    vfull[:, ATT_BLK:, :] = vc_ref[...]
    i = lax.broadcasted_iota(jnp.int32, (ATT_BLK, 2 * ATT_BLK), 0)
    kk = lax.broadcasted_iota(jnp.int32, (ATT_BLK, 2 * ATT_BLK), 1)
    band = (kk >= i) & (kk <= i + ATT_BLK)
    bias_ref[0] = jnp.where(band, 0.0, -jnp.inf)
    bias_ref[1] = jnp.where(band & (kk >= ATT_BLK), 0.0, -jnp.inf)

    def pair(p, carry):
        r = p % d
        b = p // d
        first = jnp.where((n == 0) & (b == 0), 1, 0)
        row0 = pl.multiple_of(b * ATT_BLK, ATT_BLK)
        start = b * (ATT_BLK * d) + r
        for hd in range(A_HEADS):
            sl = slice(hd * HEAD_DIM, (hd + 1) * HEAD_DIM)
            q = qc_ref[r, pl.ds(row0, ATT_BLK), sl]
            k = kfull[r, pl.ds(row0, 2 * ATT_BLK), sl]
            v = vfull[r, pl.ds(row0, 2 * ATT_BLK), sl]
            s = lax.dot_general(q, k, (((1,), (1,)), ((), ())), preferred_element_type=F32) + bias_ref[first]
            m = jnp.max(s, axis=-1, keepdims=True)
            e = jnp.exp2(s - m)
            l = jnp.sum(e, axis=-1, keepdims=True)
            o = _dot(e.astype(BF), v) / l
            lse = jnp.broadcast_to(m + jnp.log2(l), (ATT_BLK, HEAD_DIM))
            rows = pl.ds(start, ATT_BLK, stride=d) if d > 1 else pl.ds(pl.multiple_of(start, ATT_BLK), ATT_BLK)
            if chain_in:
                o2 = oin_ref[hd, rows, :]
                l2 = lin_ref[hd, rows, :]
                mx = jnp.maximum(lse, l2)
                w1, w2 = jnp.exp2(lse - mx), jnp.exp2(l2 - mx)
                den = w1 + w2
                o = (w1 * o + w2 * o2) / den
                lse = mx + jnp.log2(den)
            if final:
                y_ref[rows, sl] = o.astype(y_ref.dtype)
            else:
                o_ref[hd, rows, :] = o
                l_ref[hd, rows, :] = lse
        return carry

    lax.fori_loop(0, tile // ATT_BLK, pair, 0, unroll=2)


def _attn_group(a_g, g, S, chain):
    _, d = DIL_GROUPS[g]
    T = _att_tile(d)
    J = T // d
    final = g == 0
    assert not (final and d != 1)

    def cur(c):
        return pl.BlockSpec((None, d, J, A_WIDTH), lambda n: (n, 0, 0, c))

    def prev(c):
        return pl.BlockSpec((None, d, ATT_BLK, A_WIDTH), lambda n: (jnp.maximum(n - 1, 0), 0, J // ATT_BLK - 1, c))

    hm = pl.BlockSpec((A_HEADS, T, HEAD_DIM), lambda n: (0, n, 0))
    in_specs = [cur(0), cur(1), cur(2), prev(1), prev(2)]
    args = [a_g] * 5
    if chain is not None:
        in_specs += [hm, hm]
        args += list(chain)
    if final:
        out_specs = pl.BlockSpec((T, A_WIDTH), lambda n: (n, 0))
        out_shape = jax.ShapeDtypeStruct((S, A_WIDTH), BF)
    else:
        out_specs = [hm, hm]
        out_shape = [jax.ShapeDtypeStruct((A_HEADS, S, HEAD_DIM), F32)] * 2
    return pl.pallas_call(
        functools.partial(_attn_kernel, d=d, tile=T, chain_in=chain is not None, final=final),
        grid=(S // T,),
        in_specs=in_specs,
        out_specs=out_specs,
        out_shape=out_shape,
        scratch_shapes=[pltpu.VMEM((d, ATT_BLK + J, A_WIDTH), BF)] * 2 + [pltpu.VMEM((2, ATT_BLK, 2 * ATT_BLK), F32)],
        compiler_params=_params(("arbitrary",)),
        name=f"attn_g{g}",
    )(*args)


def _gelu_tanh(x):
    return x * (0.5 * (1.0 + jnp.tanh(math.sqrt(2.0 / math.pi) * (x + 0.044715 * (x * x * x)))))


def _sgu_kernel(h_ref, wu_ref, wv_ref, gn_ref, ws_ref, bs_ref, o_ref, *, tm):
    h = h_ref[...]
    u = _gelu_tanh(_dot(h, wu_ref[...]))
    v = _rms(_gelu_tanh(_dot(h, wv_ref[...])), gn_ref[...]).astype(BF)
    ti = lax.broadcasted_iota(jnp.int32, (SGU_CHUNK, SGU_CHUNK), 0)
    si = lax.broadcasted_iota(jnp.int32, (SGU_CHUNK, SGU_CHUNK), 1)
    for g in range(SGU_GROUPS):
        w = jnp.where(si <= ti, ws_ref[g], 0.0).astype(BF)
        cs = slice(g * SGU_GROUP_DIM, (g + 1) * SGU_GROUP_DIM)
        for c in range(tm // SGU_CHUNK):
            rs = slice(c * SGU_CHUNK, (c + 1) * SGU_CHUNK)
            mixed = _dot(w, v[rs, cs]) + bs_ref[g]
            o_ref[rs, cs] = (u[rs, cs] * mixed).astype(o_ref.dtype)


def _sgu(h, w_bf, layer, gn, ws, bs_b, tm):
    S, D = h.shape
    ub = A_COLS // B_WIDTH
    return pl.pallas_call(
        functools.partial(_sgu_kernel, tm=tm),
        grid=(S // tm,),
        in_specs=[
            pl.BlockSpec((tm, D), lambda m: (m, 0)),
            pl.BlockSpec((None, D, B_WIDTH), lambda m: (layer, 0, ub)),
            pl.BlockSpec((None, D, B_WIDTH), lambda m: (layer, 0, ub + 1)),
            pl.BlockSpec((1, B_WIDTH), lambda m: (0, 0)),
            pl.BlockSpec((SGU_GROUPS, SGU_CHUNK, SGU_CHUNK), lambda m: (0, 0, 0)),
            pl.BlockSpec((SGU_GROUPS, SGU_CHUNK, SGU_GROUP_DIM), lambda m: (0, 0, 0)),
        ],
        out_specs=pl.BlockSpec((tm, B_WIDTH), lambda m: (m, 0)),
        out_shape=jax.ShapeDtypeStruct((S, B_WIDTH), BF),
        compiler_params=_params(("parallel",)),
        name="inproj_b_sgu",
    )(h, w_bf, w_bf, gn, ws, bs_b)


def _causal_conv(ext_ref, acc, halo, w, b, taps, tm):
    ext_ref[0:HALO, :] = halo
    ext_ref[HALO:HALO + tm, :] = acc
    y = b + w[taps - 1:taps, :] * acc
    for j in range(taps - 1):
        y = y + w[j:j + 1, :] * ext_ref[HALO + j - taps + 1:HALO + j - taps + 1 + tm, :]
    return y


def _inproj_c_kernel(h_ref, wa_ref, wb_ref, wv_ref, wo_ref, wgt_ref, cw_ref, cb_ref, gb_ref,
                     qk_ref, v_ref, og_ref, gt_ref, ext_ref, halo_ref, *, tm, tn):
    m = pl.program_id(0)
    j = pl.program_id(1)
    h = h_ref[...]

    @pl.when(j == 0)
    def _():
        gt_ref[...] = _dot(h, wgt_ref[...]) + gb_ref[...]

    for i, w_ref in enumerate((wa_ref, wb_ref)):
        cs = slice(i * tn, (i + 1) * tn)
        acc = _dot(h, w_ref[...])
        halo = jnp.where(m > 0, halo_ref[j, i], 0.0)
        y = _causal_conv(ext_ref.at[i], acc, halo, cw_ref[:, cs], cb_ref[:, cs], C_CONV, tm)
        halo_ref[j, i] = acc[tm - HALO:, :]
        qk_ref[:, cs] = (y * _sigmoid(y)).astype(qk_ref.dtype)
    v_ref[...] = _dot(h, wv_ref[...]).astype(v_ref.dtype)
    og_ref[...] = _sigmoid(_dot(h, wo_ref[...])).astype(og_ref.dtype)


def _inproj_c(h, w_bf, layer, cw, cb, gate_bias, tm):
    S, D = h.shape
    tn = 512
    nj = C_WIDTH // tn
    c0 = (A_COLS + B_COLS) // tn
    qk_slabs = 2 * C_WIDTH // tn
    gate_blk = (A_COLS + B_COLS + C_MAIN) // LANE

    def wspec(fn):
        return pl.BlockSpec((None, D, tn), lambda m, j: (layer, 0, c0 + fn(j)))

    return pl.pallas_call(
        functools.partial(_inproj_c_kernel, tm=tm, tn=tn),
        grid=(S // tm, nj),
        in_specs=[
            pl.BlockSpec((tm, D), lambda m, j: (m, 0)),
            wspec(lambda j: 2 * j), wspec(lambda j: 2 * j + 1),
            wspec(lambda j: qk_slabs + j), wspec(lambda j: qk_slabs + nj + j),
            pl.BlockSpec((None, D, LANE), lambda m, j: (layer, 0, gate_blk)),
            pl.BlockSpec((None, C_CONV, 2 * tn), lambda m, j: (layer, 0, j)),
            pl.BlockSpec((None, 1, 2 * tn), lambda m, j: (layer, 0, j)),
            pl.BlockSpec((None, 1, LANE), lambda m, j: (layer, 0, 0)),
        ],
        out_specs=[pl.BlockSpec((tm, 2 * tn), lambda m, j: (m, j)),
                   pl.BlockSpec((tm, tn), lambda m, j: (m, j)),
                   pl.BlockSpec((tm, tn), lambda m, j: (m, j)),
                   pl.BlockSpec((tm, LANE), lambda m, j: (m, 0))],
        out_shape=[jax.ShapeDtypeStruct((S, 2 * C_WIDTH), BF),
                   jax.ShapeDtypeStruct((S, C_WIDTH), BF),
                   jax.ShapeDtypeStruct((S, C_WIDTH), BF),
                   jax.ShapeDtypeStruct((S, LANE), F32)],
        scratch_shapes=[pltpu.VMEM((2, tm + HALO, tn), F32), pltpu.VMEM((nj, 2, HALO, tn), F32)],
        compiler_params=_params(("arbitrary", "arbitrary")),
        name="inproj_c",
    )(h, w_bf, w_bf, w_bf, w_bf, w_bf, cw, cb, gate_bias)


def _lane_scan(x, op, lane):
    sh = 1
    while sh < C_CHUNK:
        x = jnp.where(lane >= sh, op(x, pltpu.roll(x, sh, x.ndim - 1)), x)
        sh *= 2
    return x


def _mlstm_prep_kernel(g_ref, r_ref, u_ref, inter_ref, eneg_ref, wk_ref, dec_ref, m_ref, bb_ref, rb_ref, *, nc):
    ig = g_ref[0:C_HEADS]
    fg = g_ref[C_HEADS:2 * C_HEADS]
    lane = lax.broadcasted_iota(jnp.int32, ig.shape, 2)
    lf = -(jnp.maximum(-fg, 0.0) + jnp.log1p(jnp.exp(-jnp.abs(fg))))
    b = _lane_scan(lf, jnp.add, lane)
    r = ig - b
    cm = _lane_scan(r, jnp.maximum, lane)
    shp = ig.shape
    bb_ref[...] = jnp.broadcast_to(b[:, :, C_CHUNK - 1:C_CHUNK], shp)
    rb_ref[...] = jnp.broadcast_to(cm[:, :, C_CHUNK - 1:C_CHUNK], shp)

    def step(c, m):
        m_ref[:, pl.ds(c, 1), :] = m
        return bb_ref[:, pl.ds(c, 1), :] + jnp.maximum(m, rb_ref[:, pl.ds(c, 1), :])

    lax.fori_loop(0, nc, step, jnp.zeros((C_HEADS, 1, C_CHUNK), F32))
    m_in = m_ref[...]
    bl = bb_ref[...]
    m_t = b + jnp.maximum(m_in, cm)
    m_new = jnp.broadcast_to(m_t[:, :, C_CHUNK - 1:C_CHUNK], shp)
    r_ref[...] = r
    u_ref[...] = b - m_t
    inter_ref[...] = jnp.exp(b + m_in - m_t)
    eneg_ref[...] = jnp.exp(-m_t)
    wk_ref[...] = jnp.exp(bl - b + ig - m_new)
    dec_ref[...] = jnp.exp(bl + m_in - m_new)


def _mlstm_prep(gate_rows, nc):
    shp = jax.ShapeDtypeStruct((C_HEADS, nc, C_CHUNK), F32)
    return pl.pallas_call(
        functools.partial(_mlstm_prep_kernel, nc=nc),
        out_shape=[shp] * 6,
        scratch_shapes=[pltpu.VMEM((C_HEADS, nc, C_CHUNK), F32)] * 3,
        compiler_params=pltpu.CompilerParams(vmem_limit_bytes=VMEM_LIMIT),
        name="mlstm_prep",
    )(gate_rows)


def _mlstm_kernel(q_ref, k_ref, v_ref, og_ref, r_ref, dec_ref, cf_ref, y_ref, c_ref, n_ref):
    c = pl.program_id(0)

    @pl.when(c == 0)
    def _():
        c_ref[...] = jnp.zeros_like(c_ref)
        n_ref[...] = jnp.zeros_like(n_ref)

    ti = lax.broadcasted_iota(jnp.int32, (C_CHUNK, C_CHUNK), 0)
    si = lax.broadcasted_iota(jnp.int32, (C_CHUNK, C_CHUNK), 1)
    causal = si <= ti
    kscale = C_HEAD_DIM ** -0.5
    cf = cf_ref[...]
    for h in range(C_HEADS):
        sl = slice(h * C_HEAD_DIM, (h + 1) * C_HEAD_DIM)
        q, k, v = q_ref[:, sl], k_ref[:, sl], v_ref[:, sl]
        u = cf[:, 4 * h:4 * h + 1]
        inter = cf[:, 4 * h + 1:4 * h + 2]
        eneg = cf[:, 4 * h + 2:4 * h + 3]
        wk = cf[:, 4 * h + 3:4 * h + 4]
        r = r_ref[h:h + 1, :]
        dec = dec_ref[h:h + 1, 0:1]
        s = lax.dot_general(q, k, (((1,), (1,)), ((), ())), preferred_element_type=F32) * kscale
        sc = s * jnp.exp(jnp.where(causal, u + r, -jnp.inf))
        cmat = c_ref[h]
        nvec = n_ref[h]
        num = _dot(sc.astype(BF), v) + inter * _dot(q, cmat.astype(BF))
        qn = jnp.sum(q.astype(F32) * nvec, axis=-1, keepdims=True)
        den = jnp.sum(sc, axis=-1, keepdims=True) + inter * qn
        hh = num / jnp.maximum(jnp.abs(den), eneg)
        y_ref[:, sl] = (og_ref[:, sl].astype(F32) * hh).astype(y_ref.dtype)
        kw = k.astype(F32) * (wk * kscale)
        upd = lax.dot_general(kw.astype(BF), v, (((0,), (0,)), ((), ())), preferred_element_type=F32)
        c_ref[h] = dec * cmat + upd
        n_ref[h] = dec * nvec + jnp.sum(kw, axis=0, keepdims=True)


def _mlstm(qk, v, og, r_rows, dec_rows, colfeat, S):
    nc = S // C_CHUNK
    row = pl.BlockSpec((C_CHUNK, C_WIDTH), lambda c: (c, 0))
    return pl.pallas_call(
        _mlstm_kernel,
        grid=(nc,),
        in_specs=[
            row, pl.BlockSpec((C_CHUNK, C_WIDTH), lambda c: (c, 1)), row, row,
            pl.BlockSpec((None, C_HEADS, C_CHUNK), lambda c: (c, 0, 0)),
            pl.BlockSpec((None, C_HEADS, C_CHUNK), lambda c: (c, 0, 0)),
            pl.BlockSpec((C_CHUNK, 4 * C_HEADS), lambda c: (c, 0)),
        ],
        out_specs=row,
        out_shape=jax.ShapeDtypeStruct((S, C_WIDTH), BF),
        scratch_shapes=[pltpu.VMEM((C_HEADS, C_HEAD_DIM, C_HEAD_DIM), F32), pltpu.VMEM((C_HEADS, 1, C_HEAD_DIM), F32)],
        compiler_params=_params(("arbitrary",)),
        name="mlstm",
    )(qk, qk, v, og, r_rows, dec_rows, colfeat)


def _merge_kernel(h_ref, wga_ref, wgb_ref, wgc_ref, ya_ref, yb_ref, yc_ref, pa_ref, pb_ref, pc_ref, o_ref):
    h = h_ref[...]

    def branch(wg_ref, y_ref, p_ref):
        return _sigmoid(_dot(h, wg_ref[...])) * _dot(y_ref[...], p_ref[...])

    merged = branch(wga_ref, ya_ref, pa_ref) + branch(wgb_ref, yb_ref, pb_ref) + branch(wgc_ref, yc_ref, pc_ref)
    o_ref[...] = merged.astype(o_ref.dtype)


def _merge(h, wg, layer, ya, yb, yc, pa, pb, pc, tm):
    S, D = h.shape
    tn = 512
    nt = D // tn

    def gspec(b):
        return pl.BlockSpec((None, D, tn), lambda m, n: (layer, 0, b * nt + n))

    def yspec(w):
        return pl.BlockSpec((tm, w), lambda m, n: (m, 0))

    def pspec(w):
        return pl.BlockSpec((None, w, tn), lambda m, n: (layer, 0, n))

    return pl.pallas_call(
        _merge_kernel,
        grid=(S // tm, nt),
        in_specs=[pl.BlockSpec((tm, D), lambda m, n: (m, 0)), gspec(0), gspec(1), gspec(2),
                  yspec(A_WIDTH), yspec(B_WIDTH), yspec(C_WIDTH), pspec(A_WIDTH), pspec(B_WIDTH), pspec(C_WIDTH)],
        out_specs=pl.BlockSpec((tm, tn), lambda m, n: (m, n)),
        out_shape=jax.ShapeDtypeStruct((S, D), BF),
        compiler_params=_params(("parallel", "arbitrary")),
        name="merge",
    )(h, wg, wg, wg, ya, yb, yc, pa, pb, pc)


def _outproj_kernel(mg_ref, w_ref, x_ref, g_ref, xo_ref, ho_ref):
    xn = x_ref[...] + _dot(mg_ref[...], w_ref[...])
    xo_ref[...] = xn
    ho_ref[...] = _rms(xn, g_ref[...]).astype(ho_ref.dtype)


def _outproj(mg, wout, layer, x, g2, tm):
    S, D = x.shape
    return pl.pallas_call(
        _outproj_kernel,
        grid=(S // tm,),
        in_specs=[
            pl.BlockSpec((tm, D), lambda m: (m, 0)),
            _resident((None, D, D), lambda m: (layer, 0, 0)),
            pl.BlockSpec((tm, D), lambda m: (m, 0)),
            pl.BlockSpec((None, 1, D), lambda m: (layer, 0, 0)),
        ],
        out_specs=[pl.BlockSpec((tm, D), lambda m: (m, 0)), pl.BlockSpec((tm, D), lambda m: (m, 0))],
        out_shape=[jax.ShapeDtypeStruct((S, D), F32), jax.ShapeDtypeStruct((S, D), BF)],
        compiler_params=_params(("parallel",)),
        name="outproj",
    )(mg, wout, x, g2)


def _ffn_up_kernel(h_ref, wg_ref, wu_ref, cwg_ref, cwu_ref, cbg_ref, cbu_ref, o_ref, ext_ref, halo_ref, *, tm):
    m = pl.program_id(0)
    n = pl.program_id(1)
    h = h_ref[...]
    outs = []
    for i, (w_ref, cw_ref, cb_ref) in enumerate(((wg_ref, cwg_ref, cbg_ref), (wu_ref, cwu_ref, cbu_ref))):
        acc = _dot(h, w_ref[...])
        halo = jnp.where(m > 0, halo_ref[n, i], 0.0)
        outs.append(_causal_conv(ext_ref.at[i], acc, halo, cw_ref[...], cb_ref[...], FFN_CONV, tm))
        halo_ref[n, i] = acc[tm - HALO:, :]
    a_gate, a_up = outs
    o_ref[...] = (a_gate * _sigmoid(a_gate) * a_up).astype(o_ref.dtype)


def _ffn_up(h, wup, layer, cw, cb, tm):
    S, D = h.shape
    tn = 512
    nt = D_FF // tn

    def wspec(off):
        return pl.BlockSpec((None, D, tn), lambda m, n: (layer, 0, off + n))

    def cspec(rows, off):
        return pl.BlockSpec((None, rows, tn), lambda m, n: (layer, 0, off + n))

    return pl.pallas_call(
        functools.partial(_ffn_up_kernel, tm=tm),
        grid=(S // tm, nt),
        in_specs=[pl.BlockSpec((tm, D), lambda m, n: (m, 0)), wspec(0), wspec(nt),
                  cspec(FFN_CONV, 0), cspec(FFN_CONV, nt), cspec(1, 0), cspec(1, nt)],
        out_specs=pl.BlockSpec((tm, tn), lambda m, n: (m, n)),
        out_shape=jax.ShapeDtypeStruct((S, D_FF), BF),
        scratch_shapes=[pltpu.VMEM((2, tm + HALO, tn), F32), pltpu.VMEM((nt, 2, HALO, tn), F32)],
        compiler_params=_params(("arbitrary", "arbitrary")),
        name="ffn_up",
    )(h, wup, wup, cw, cw, cb, cb)


def _ffn_down_kernel(a_ref, w_ref, x_ref, g_ref, *out_refs, last):
    xn = x_ref[...] + _dot(a_ref[...], w_ref[...])
    if last:
        out_refs[0][...] = _rms(xn, g_ref[...])
    else:
        out_refs[0][...] = xn
        out_refs[1][...] = _rms(xn, g_ref[...]).astype(out_refs[1].dtype)


def _ffn_down(a, wdown, layer, x, g, last, tm):
    S, D = x.shape
    row = pl.BlockSpec((tm, D), lambda m: (m, 0))
    if last:
        out_specs, out_shape = row, jax.ShapeDtypeStruct((S, D), F32)
    else:
        out_specs = [row, row]
        out_shape = [jax.ShapeDtypeStruct((S, D), F32), jax.ShapeDtypeStruct((S, D), BF)]
    return pl.pallas_call(
        functools.partial(_ffn_down_kernel, last=last),
        grid=(S // tm,),
        in_specs=[
            pl.BlockSpec((tm, D_FF), lambda m: (m, 0)),
            _resident((None, D_FF, D), lambda m: (layer, 0, 0)),
            row,
            pl.BlockSpec((1, D), lambda m: (0, 0)),
        ],
        out_specs=out_specs,
        out_shape=out_shape,
        compiler_params=_params(("parallel",)),
        name="ffn_down",
    )(a, wdown, x, g)


def _rope_lane_tables(S):
    half = ROT_DIM // 2
    inv = jnp.power(jnp.float32(ROPE_THETA), -jnp.arange(half, dtype=F32) * (2.0 / ROT_DIM))
    ang = jnp.arange(S, dtype=F32)[:, None] * inv[None, :]
    cos, sin = jnp.cos(ang), jnp.sin(ang)
    zeros = jnp.zeros((S, HEAD_DIM - ROT_DIM), F32)
    zh = jnp.zeros((S, half), F32)
    cosf = jnp.concatenate([cos, cos, jnp.ones((S, HEAD_DIM - ROT_DIM), F32)], axis=1)
    sina = jnp.concatenate([-sin, zh, zeros], axis=1)
    sinb = jnp.concatenate([zh, sin, zeros], axis=1)
    return cosf, sina, sinb


def kernel(x, norm1_g, w_in, conv_qk_w, conv_qk_b, b_igate, b_fgate, sgu_norm_g, sgu_w, sgu_b, w_branch_a, w_branch_b, w_branch_c, w_out, norm2_g, w_up, ffn_conv_w, ffn_conv_b, w_down, final_norm_g):
    B, S, D = x.shape
    depth = w_in.shape[0]
    assert B == 1 and D == D_MODEL and S % DIL_GROUPS[-1][0] == 0
    nc = S // C_CHUNK
    tm = min(1024, S)
    tm_row = min(512, S)
    tm_down = min(256, S)

    o_gate = A_COLS + B_COLS + C_MAIN
    o_g = o_gate + C_GATES
    w_bf = w_in[:, :, :o_gate + LANE].astype(BF)
    wg = w_in[:, :, o_g:].astype(BF)
    gate_bias = jnp.pad(jnp.concatenate([b_igate, b_fgate], axis=1), ((0, 0), (0, LANE - C_GATES)))[:, None, :].astype(F32)
    pa, pb, pc = w_branch_a.astype(BF), w_branch_b.astype(BF), w_branch_c.astype(BF)
    wout, wup, wdown = w_out.astype(BF), w_up.astype(BF), w_down.astype(BF)
    sgu_b_b = jnp.broadcast_to(sgu_b[:, :, :, None], sgu_b.shape + (SGU_GROUP_DIM,)).astype(F32)
    cosf, sina, sinb = _rope_lane_tables(S)

    xs = x[0]
    h = _rmsnorm(xs, norm1_g[0][None, :], tm_row)
    for l in range(depth):
        chain = None
        for g in reversed(range(N_DIL)):
            a_g = _inproj_a(h, w_bf, l, g, cosf, sina, sinb, tm)
            chain = _attn_group(a_g, g, S, chain)
        y_a = chain
        y_b = _sgu(h, w_bf, l, sgu_norm_g[l][None, :], sgu_w[l], sgu_b_b[l], tm_row)
        qk_c, v_c, og_c, gts = _inproj_c(h, w_bf, l, conv_qk_w, conv_qk_b[:, None, :], gate_bias, tm)
        gate_rows = gts[:, :C_GATES].T.reshape(C_GATES, nc, C_CHUNK)
        r, u, inter, eneg, wk, dec = _mlstm_prep(gate_rows, nc)
        colfeat = jnp.stack([u, inter, eneg, wk], axis=1).transpose(2, 3, 0, 1).reshape(S, 4 * C_HEADS)
        y_c = _mlstm(qk_c, v_c, og_c, r.transpose(1, 0, 2), dec.transpose(1, 0, 2), colfeat, S)
        mg = _merge(h, wg, l, y_a, y_b, y_c, pa, pb, pc, tm)
        xs, h2 = _outproj(mg, wout, l, xs, norm2_g[:, None, :], tm_row)
        act = _ffn_up(h2, wup, l, ffn_conv_w, ffn_conv_b[:, None, :], tm)
        if l + 1 < depth:
            xs, h = _ffn_down(act, wdown, l, xs, norm1_g[l + 1][None, :], False, tm_down)
        else:
            xs = _ffn_down(act, wdown, l, xs, final_norm_g[None, :], True, tm_down)
    return xs[None]
```

```python
import functools
import math

import jax
import jax.numpy as jnp
from jax import lax
from jax.experimental import pallas as pl
from jax.experimental.pallas import tpu as pltpu

BF = jnp.bfloat16
F32 = jnp.float32

D_MODEL = 2048
HEAD_DIM = 128
ROT_DIM = HEAD_DIM // 4
ROPE_THETA = 500000.0
DIL_GROUPS = ((128, 1), (512, 4), (2048, 16))
N_DIL = len(DIL_GROUPS)
A_HEADS = 4
A_WIDTH = A_HEADS * HEAD_DIM
ATT_BLK = 128
ATT_TILE = 1024
SGU_CHUNK = 128
SGU_GROUPS = 4
SGU_GROUP_DIM = 128
B_WIDTH = SGU_GROUPS * SGU_GROUP_DIM
C_HEADS = 4
C_HEAD_DIM = 256
C_WIDTH = C_HEADS * C_HEAD_DIM
C_CHUNK = 128
C_SUB = 2
C_CONV = 4
N_BRANCH = 3
D_FF = 5632
FFN_CONV = 3
EPS = 1e-6

A_COLS = 3 * N_DIL * A_WIDTH
B_COLS = 2 * B_WIDTH
C_MAIN = 4 * C_WIDTH
C_GATES = 2 * C_HEADS
G_COLS = N_BRANCH * D_MODEL

LANE = 128
LOG2E = math.log2(math.e)
HALO = 8
VMEM_LIMIT = 56 * 1024 * 1024


def _params(sem):
    return pltpu.CompilerParams(dimension_semantics=sem, vmem_limit_bytes=VMEM_LIMIT)


def _resident(block_shape, index_map):
    return pl.BlockSpec(block_shape, index_map, pipeline_mode=pl.Buffered(1))


def _sigmoid(x):
    return 1.0 / (1.0 + jnp.exp(-x))


def _rms(x, g):
    return x * lax.rsqrt(jnp.mean(x * x, axis=-1, keepdims=True) + EPS) * g


def _dot(a, b):
    return jnp.dot(a, b, preferred_element_type=F32)


def _att_tile(d):
    return max(ATT_TILE, ATT_BLK * d)


def _cast_kernel(x_ref, o_ref):
    o_ref[...] = x_ref[...].astype(o_ref.dtype)


def _cast_leading_cols(w, ncols, tr):
    L, D, _ = w.shape
    spec = pl.BlockSpec((None, tr, ncols), lambda l, r: (l, r, 0))
    return pl.pallas_call(
        _cast_kernel,
        grid=(L, D // tr),
        in_specs=[spec],
        out_specs=spec,
        out_shape=jax.ShapeDtypeStruct((L, D, ncols), BF),
        compiler_params=_params(("parallel", "parallel")),
        name="cast_w_in",
    )(w)


def _cast_shift_kernel(a_ref, b_ref, o_ref, *, shift, tn):
    lane = lax.broadcasted_iota(jnp.int32, (a_ref.shape[0], LANE), 1)
    parts = [a_ref[:, j * LANE:(j + 1) * LANE] for j in range(tn // LANE)] + [b_ref[...]]
    rolled = [pltpu.roll(p, LANE - shift, 1) for p in parts]
    for j in range(tn // LANE):
        o_ref[:, j * LANE:(j + 1) * LANE] = jnp.where(lane < LANE - shift, rolled[j], rolled[j + 1]).astype(o_ref.dtype)


def _cast_trailing_cols(w, start, ncols):
    L, D, _ = w.shape
    tn = 512
    shift = start % LANE
    assert 0 < shift and (start - shift) % tn == 0 and ncols % tn == 0
    blk0 = (start - shift) // tn
    return pl.pallas_call(
        functools.partial(_cast_shift_kernel, shift=shift, tn=tn),
        grid=(L, ncols // tn),
        in_specs=[pl.BlockSpec((None, D, tn), lambda l, n: (l, 0, blk0 + n)),
                  pl.BlockSpec((None, D, LANE), lambda l, n: (l, 0, (blk0 + n + 1) * (tn // LANE)))],
        out_specs=pl.BlockSpec((None, D, tn), lambda l, n: (l, 0, n)),
        out_shape=jax.ShapeDtypeStruct((L, D, ncols), BF),
        compiler_params=_params(("parallel", "parallel")),
        name="cast_w_gates",
    )(w, w)


def _rmsnorm_kernel(x_ref, g_ref, o_ref):
    o_ref[...] = _rms(x_ref[...], g_ref[...]).astype(o_ref.dtype)


def _rmsnorm(x, g, tm):
    S, D = x.shape
    return pl.pallas_call(
        _rmsnorm_kernel,
        grid=(S // tm,),
        in_specs=[pl.BlockSpec((tm, D), lambda m: (m, 0)), pl.BlockSpec((1, D), lambda m: (0, 0))],
        out_specs=pl.BlockSpec((tm, D), lambda m: (m, 0)),
        out_shape=jax.ShapeDtypeStruct((S, D), BF),
        compiler_params=_params(("parallel",)),
        name="rmsnorm",
    )(x, g)


def _inproj_a_kernel(h_ref, wq_ref, wk_ref, wv_ref, cos_ref, sa_ref, sb_ref, o_ref, *scratch, d, tm):
    h = h_ref[...]
    c, sa, sb = cos_ref[...], sa_ref[...], sb_ref[...]
    for t, w_ref in enumerate((wq_ref, wk_ref, wv_ref)):
        acc = _dot(h, w_ref[...])
        for j in range(A_HEADS):
            xj = acc[:, j * HEAD_DIM:(j + 1) * HEAD_DIM]
            if t < 2:
                xj = xj * c + pltpu.roll(xj, HEAD_DIM - ROT_DIM // 2, 1) * sa + pltpu.roll(xj, ROT_DIM // 2, 1) * sb
            if t == 0:
                xj = xj * (HEAD_DIM ** -0.5 * LOG2E)
            cols = slice(t * A_WIDTH + j * HEAD_DIM, t * A_WIDTH + (j + 1) * HEAD_DIM)
            if d == 1:
                o_ref[0, :, cols] = xj.astype(o_ref.dtype)
            else:
                scr = scratch[0]
                scr[t, j] = xj
                for r in range(d):
                    o_ref[r, :, cols] = scr[t, j, pl.ds(r, tm // d, stride=d), :].astype(o_ref.dtype)


def _inproj_a(h, wa, layer, g, cosf, sina, sinb, tm):
    S, D = h.shape
    _, d = DIL_GROUPS[g]
    T = _att_tile(d)
    per = T // tm

    def wspec(t):
        return pl.BlockSpec((None, D, A_WIDTH), lambda m: (layer, 0, t * N_DIL + g))

    tab = pl.BlockSpec((tm, HEAD_DIM), lambda m: (m, 0))
    return pl.pallas_call(
        functools.partial(_inproj_a_kernel, d=d, tm=tm),
        grid=(S // tm,),
        in_specs=[pl.BlockSpec((tm, D), lambda m: (m, 0)), wspec(0), wspec(1), wspec(2), tab, tab, tab],
        out_specs=pl.BlockSpec((None, d, tm // d, 3 * A_WIDTH), lambda m: (m // per, 0, m % per, 0)),
        out_shape=jax.ShapeDtypeStruct((S // T, d, T // d, 3 * A_WIDTH), BF),
        scratch_shapes=[] if d == 1 else [pltpu.VMEM((3, A_HEADS, tm, HEAD_DIM), F32)],
        compiler_params=_params(("parallel",)),
        name=f"inproj_a_g{g}",
    )(h, wa, wa, wa, cosf, sina, sinb)


def _attn_kernel(*refs, d, tile, chain_in, final):
    qc_ref, kc_ref, vc_ref, kp_ref, vp_ref = refs[:5]
    pos = 5
    if chain_in:
        oin_ref, lin_ref = refs[pos:pos + 2]
        pos += 2
    if final:
        y_ref = refs[pos]
        pos += 1
    else:
        o_ref, l_ref = refs[pos:pos + 2]
        pos += 2
    kfull, vfull, bias_ref = refs[pos:pos + 3]

    n = pl.program_id(0)
    kfull[:, 0:ATT_BLK, :] = kp_ref[...]
    kfull[:, ATT_BLK:, :] = kc_ref[...]
    vfull[:, 0:ATT_BLK, :] = vp_ref[...]
    vfull[:, ATT_BLK:, :] = vc_ref[...]
    i = lax.broadcasted_iota(jnp.int32, (ATT_BLK, 2 * ATT_BLK), 0)
    kk = lax.broadcasted_iota(jnp.int32, (ATT_BLK, 2 * ATT_BLK), 1)
    band = (kk >= i) & (kk <= i + ATT_BLK)
    bias_ref[0] = jnp.where(band, 0.0, -jnp.inf)
    bias_ref[1] = jnp.where(band & (kk >= ATT_BLK), 0.0, -jnp.inf)

    def pair(p, carry):
        r = p % d
        b = p // d
        first = jnp.where((n == 0) & (b == 0), 1, 0)
        row0 = pl.multiple_of(b * ATT_BLK, ATT_BLK)
        start = b * (ATT_BLK * d) + r
        for hd in range(A_HEADS):
            sl = slice(hd * HEAD_DIM, (hd + 1) * HEAD_DIM)
            q = qc_ref[r, pl.ds(row0, ATT_BLK), sl]
            k = kfull[r, pl.ds(row0, 2 * ATT_BLK), sl]
            v = vfull[r, pl.ds(row0, 2 * ATT_BLK), sl]
            s = lax.dot_general(q, k, (((1,), (1,)), ((), ())), preferred_element_type=F32) + bias_ref[first]
            m = jnp.max(s, axis=-1, keepdims=True)
            e = jnp.exp2(s - m)
            l = jnp.sum(e, axis=-1, keepdims=True)
            o = _dot(e.astype(BF), v) / l
            lse = jnp.broadcast_to(m + jnp.log2(l), (ATT_BLK, HEAD_DIM))
            rows = pl.ds(start, ATT_BLK, stride=d) if d > 1 else pl.ds(pl.multiple_of(start, ATT_BLK), ATT_BLK)
            if chain_in:
                o2 = oin_ref[hd, rows, :]
                l2 = lin_ref[hd, rows, :]
                mx = jnp.maximum(lse, l2)
                w1, w2 = jnp.exp2(lse - mx), jnp.exp2(l2 - mx)
                den = w1 + w2
                o = (w1 * o + w2 * o2) / den
                lse = mx + jnp.log2(den)
            if final:
                y_ref[rows, sl] = o.astype(y_ref.dtype)
            else:
                o_ref[hd, rows, :] = o
                l_ref[hd, rows, :] = lse
        return carry

    lax.fori_loop(0, tile // ATT_BLK, pair, 0, unroll=True)


def _attn_group(a_g, g, S, chain):
    _, d = DIL_GROUPS[g]
    T = _att_tile(d)
    J = T // d
    final = g == 0
    assert not (final and d != 1)

    def cur(c):
        return pl.BlockSpec((None, d, J, A_WIDTH), lambda n: (n, 0, 0, c))

    def prev(c):
        return pl.BlockSpec((None, d, ATT_BLK, A_WIDTH), lambda n: (jnp.maximum(n - 1, 0), 0, J // ATT_BLK - 1, c))

    hm = pl.BlockSpec((A_HEADS, T, HEAD_DIM), lambda n: (0, n, 0))
    in_specs = [cur(0), cur(1), cur(2), prev(1), prev(2)]
    args = [a_g] * 5
    if chain is not None:
        in_specs += [hm, hm]
        args += list(chain)
    if final:
        out_specs = pl.BlockSpec((T, A_WIDTH), lambda n: (n, 0))
        out_shape = jax.ShapeDtypeStruct((S, A_WIDTH), BF)
    else:
        out_specs = [hm, hm]
        out_shape = [jax.ShapeDtypeStruct((A_HEADS, S, HEAD_DIM), F32)] * 2
    return pl.pallas_call(
        functools.partial(_attn_kernel, d=d, tile=T, chain_in=chain is not None, final=final),
        grid=(S // T,),
        in_specs=in_specs,
        out_specs=out_specs,
        out_shape=out_shape,
        scratch_shapes=[pltpu.VMEM((d, ATT_BLK + J, A_WIDTH), BF)] * 2 + [pltpu.VMEM((2, ATT_BLK, 2 * ATT_BLK), F32)],
        compiler_params=_params(("arbitrary",)),
        name=f"attn_g{g}",
    )(*args)


def _gelu_tanh(x):
    return x * (0.5 * (1.0 + jnp.tanh(math.sqrt(2.0 / math.pi) * (x + 0.044715 * (x * x * x)))))


def _sgu_kernel(h_ref, wu_ref, wv_ref, gn_ref, ws_ref, bs_ref, o_ref, *, tm):
    h = h_ref[...]
    u = _gelu_tanh(_dot(h, wu_ref[...]))
    v = _rms(_gelu_tanh(_dot(h, wv_ref[...])), gn_ref[...]).astype(BF)
    ti = lax.broadcasted_iota(jnp.int32, (SGU_CHUNK, SGU_CHUNK), 0)
    si = lax.broadcasted_iota(jnp.int32, (SGU_CHUNK, SGU_CHUNK), 1)
    for g in range(SGU_GROUPS):
        w = jnp.where(si <= ti, ws_ref[g], 0.0).astype(BF)
        cs = slice(g * SGU_GROUP_DIM, (g + 1) * SGU_GROUP_DIM)
        for c in range(tm // SGU_CHUNK):
            rs = slice(c * SGU_CHUNK, (c + 1) * SGU_CHUNK)
            mixed = _dot(w, v[rs, cs]) + bs_ref[g]
            o_ref[rs, cs] = (u[rs, cs] * mixed).astype(o_ref.dtype)


def _sgu(h, w_bf, layer, gn, ws, bs_b, tm):
    S, D = h.shape
    ub = A_COLS // B_WIDTH
    return pl.pallas_call(
        functools.partial(_sgu_kernel, tm=tm),
        grid=(S // tm,),
        in_specs=[
            pl.BlockSpec((tm, D), lambda m: (m, 0)),
            pl.BlockSpec((None, D, B_WIDTH), lambda m: (layer, 0, ub)),
            pl.BlockSpec((None, D, B_WIDTH), lambda m: (layer, 0, ub + 1)),
            pl.BlockSpec((1, B_WIDTH), lambda m: (0, 0)),
            pl.BlockSpec((SGU_GROUPS, SGU_CHUNK, SGU_CHUNK), lambda m: (0, 0, 0)),
            pl.BlockSpec((SGU_GROUPS, SGU_CHUNK, SGU_GROUP_DIM), lambda m: (0, 0, 0)),
        ],
        out_specs=pl.BlockSpec((tm, B_WIDTH), lambda m: (m, 0)),
        out_shape=jax.ShapeDtypeStruct((S, B_WIDTH), BF),
        compiler_params=_params(("parallel",)),
        name="inproj_b_sgu",
    )(h, w_bf, w_bf, gn, ws, bs_b)


def _causal_conv(ext_ref, acc, halo, w, b, taps, tm):
    ext_ref[0:HALO, :] = halo
    ext_ref[HALO:HALO + tm, :] = acc
    y = b + w[taps - 1:taps, :] * acc
    for j in range(taps - 1):
        y = y + w[j:j + 1, :] * ext_ref[HALO + j - taps + 1:HALO + j - taps + 1 + tm, :]
    return y


def _inproj_c_kernel(h_ref, wa_ref, wb_ref, wv_ref, wo_ref, wgt_ref, cw_ref, cb_ref, gb_ref,
                     qk_ref, v_ref, og_ref, gt_ref, ext_ref, halo_ref, *, tm, tn):
    m = pl.program_id(0)
    j = pl.program_id(1)
    h = h_ref[...]

    @pl.when(j == 0)
    def _():
        gt_ref[...] = _dot(h, wgt_ref[...]) + gb_ref[...]

    for i, w_ref in enumerate((wa_ref, wb_ref)):
        cs = slice(i * tn, (i + 1) * tn)
        acc = _dot(h, w_ref[...])
        halo = jnp.where(m > 0, halo_ref[j, i], 0.0)
        y = _causal_conv(ext_ref.at[i], acc, halo, cw_ref[:, cs], cb_ref[:, cs], C_CONV, tm)
        halo_ref[j, i] = acc[tm - HALO:, :]
        qk_ref[:, cs] = (y * _sigmoid(y)).astype(qk_ref.dtype)
    v_ref[...] = _dot(h, wv_ref[...]).astype(v_ref.dtype)
    og_ref[...] = _sigmoid(_dot(h, wo_ref[...])).astype(og_ref.dtype)


def _inproj_c(h, w_bf, layer, cw, cb, gate_bias, tm):
    S, D = h.shape
    tn = 512
    nj = C_WIDTH // tn
    c0 = (A_COLS + B_COLS) // tn
    qk_slabs = 2 * C_WIDTH // tn
    gate_blk = (A_COLS + B_COLS + C_MAIN) // LANE

    def wspec(fn):
        return pl.BlockSpec((None, D, tn), lambda m, j: (layer, 0, c0 + fn(j)))

    return pl.pallas_call(
        functools.partial(_inproj_c_kernel, tm=tm, tn=tn),
        grid=(S // tm, nj),
        in_specs=[
            pl.BlockSpec((tm, D), lambda m, j: (m, 0)),
            wspec(lambda j: 2 * j), wspec(lambda j: 2 * j + 1),
            wspec(lambda j: qk_slabs + j), wspec(lambda j: qk_slabs + nj + j),
            pl.BlockSpec((None, D, LANE), lambda m, j: (layer, 0, gate_blk)),
            pl.BlockSpec((None, C_CONV, 2 * tn), lambda m, j: (layer, 0, j)),
            pl.BlockSpec((None, 1, 2 * tn), lambda m, j: (layer, 0, j)),
            pl.BlockSpec((None, 1, LANE), lambda m, j: (layer, 0, 0)),
        ],
        out_specs=[pl.BlockSpec((tm, 2 * tn), lambda m, j: (m, j)),
                   pl.BlockSpec((tm, tn), lambda m, j: (m, j)),
                   pl.BlockSpec((tm, tn), lambda m, j: (m, j)),
                   pl.BlockSpec((tm, LANE), lambda m, j: (m, 0))],
        out_shape=[jax.ShapeDtypeStruct((S, 2 * C_WIDTH), BF),
                   jax.ShapeDtypeStruct((S, C_WIDTH), BF),
                   jax.ShapeDtypeStruct((S, C_WIDTH), BF),
                   jax.ShapeDtypeStruct((S, LANE), F32)],
        scratch_shapes=[pltpu.VMEM((2, tm + HALO, tn), F32), pltpu.VMEM((nj, 2, HALO, tn), F32)],
        compiler_params=_params(("arbitrary", "arbitrary")),
        name="inproj_c",
    )(h, w_bf, w_bf, w_bf, w_bf, w_bf, cw, cb, gate_bias)


def _lane_scan(x, op, lane):
    sh = 1
    while sh < C_CHUNK:
        x = jnp.where(lane >= sh, op(x, pltpu.roll(x, sh, x.ndim - 1)), x)
        sh *= 2
    return x


def _mlstm_prep_kernel(g_ref, r_ref, u_ref, inter_ref, eneg_ref, wk_ref, dec_ref, m_ref, bb_ref, rb_ref, *, nc):
    ig = g_ref[0:C_HEADS]
    fg = g_ref[C_HEADS:2 * C_HEADS]
    lane = lax.broadcasted_iota(jnp.int32, ig.shape, 2)
    lf = -(jnp.maximum(-fg, 0.0) + jnp.log1p(jnp.exp(-jnp.abs(fg))))
    b = _lane_scan(lf, jnp.add, lane)
    r = ig - b
    cm = _lane_scan(r, jnp.maximum, lane)
    shp = ig.shape
    bb_ref[...] = jnp.broadcast_to(b[:, :, C_CHUNK - 1:C_CHUNK], shp)
    rb_ref[...] = jnp.broadcast_to(cm[:, :, C_CHUNK - 1:C_CHUNK], shp)

    def step(c, m):
        m_ref[:, pl.ds(c, 1), :] = m
        return bb_ref[:, pl.ds(c, 1), :] + jnp.maximum(m, rb_ref[:, pl.ds(c, 1), :])

    lax.fori_loop(0, nc, step, jnp.zeros((C_HEADS, 1, C_CHUNK), F32))
    m_in = m_ref[...]
    bl = bb_ref[...]
    m_t = b + jnp.maximum(m_in, cm)
    m_new = jnp.broadcast_to(m_t[:, :, C_CHUNK - 1:C_CHUNK], shp)
    r_ref[...] = r
    u_ref[...] = b - m_t
    inter_ref[...] = jnp.exp(b + m_in - m_t)
    eneg_ref[...] = jnp.exp(-m_t)
    wk_ref[...] = jnp.exp(bl - b + ig - m_new)
    dec_ref[...] = jnp.exp(bl + m_in - m_new)


def _mlstm_prep(gate_rows, nc):
    shp = jax.ShapeDtypeStruct((C_HEADS, nc, C_CHUNK), F32)
    return pl.pallas_call(
        functools.partial(_mlstm_prep_kernel, nc=nc),
        out_shape=[shp] * 6,
        scratch_shapes=[pltpu.VMEM((C_HEADS, nc, C_CHUNK), F32)] * 3,
        compiler_params=pltpu.CompilerParams(vmem_limit_bytes=VMEM_LIMIT),
        name="mlstm_prep",
    )(gate_rows)


def _mlstm_kernel(q_ref, k_ref, v_ref, og_ref, r_ref, dec_ref, cf_ref, y_ref, c_ref, n_ref, *, nsub):
    c = pl.program_id(0)

    @pl.when(c == 0)
    def _():
        c_ref[...] = jnp.zeros_like(c_ref)
        n_ref[...] = jnp.zeros_like(n_ref)

    ti = lax.broadcasted_iota(jnp.int32, (C_CHUNK, C_CHUNK), 0)
    si = lax.broadcasted_iota(jnp.int32, (C_CHUNK, C_CHUNK), 1)
    causal = si <= ti
    kscale = C_HEAD_DIM ** -0.5
    cfs = [cf_ref[j * C_CHUNK:(j + 1) * C_CHUNK, :] for j in range(nsub)]
    for h in range(C_HEADS):
        sl = slice(h * C_HEAD_DIM, (h + 1) * C_HEAD_DIM)
        cmat = c_ref[h]
        nvec = n_ref[h]
        for j in range(nsub):
            rs = slice(j * C_CHUNK, (j + 1) * C_CHUNK)
            q, k, v = q_ref[rs, sl], k_ref[rs, sl], v_ref[rs, sl]
            cf = cfs[j]
            u, inter = cf[:, 4 * h:4 * h + 1], cf[:, 4 * h + 1:4 * h + 2]
            eneg, wk = cf[:, 4 * h + 2:4 * h + 3], cf[:, 4 * h + 3:4 * h + 4]
            r = r_ref[j, h:h + 1, :]
            dec = dec_ref[j, h:h + 1, 0:1]
            s = lax.dot_general(q, k, (((1,), (1,)), ((), ())), preferred_element_type=F32) * kscale
            sc = s * jnp.exp(jnp.where(causal, u + r, -jnp.inf))
            num = _dot(sc.astype(BF), v) + inter * _dot(q, cmat.astype(BF))
            qn = jnp.sum(q.astype(F32) * nvec, axis=-1, keepdims=True)
            den = jnp.sum(sc, axis=-1, keepdims=True) + inter * qn
            hh = num / jnp.maximum(jnp.abs(den), eneg)
            y_ref[rs, sl] = (og_ref[rs, sl].astype(F32) * hh).astype(y_ref.dtype)
            kw = k.astype(F32) * (wk * kscale)
            upd = lax.dot_general(kw.astype(BF), v, (((0,), (0,)), ((), ())), preferred_element_type=F32)
            cmat = dec * cmat + upd
            nvec = dec * nvec + jnp.sum(kw, axis=0, keepdims=True)
        c_ref[h] = cmat
        n_ref[h] = nvec


def _mlstm(qk, v, og, r_rows, dec_rows, colfeat, S, nsub):
    nc = S // C_CHUNK
    rows = nsub * C_CHUNK
    row = pl.BlockSpec((rows, C_WIDTH), lambda c: (c, 0))
    gate = pl.BlockSpec((nsub, C_HEADS, C_CHUNK), lambda c: (c, 0, 0))
    return pl.pallas_call(
        functools.partial(_mlstm_kernel, nsub=nsub),
        grid=(nc // nsub,),
        in_specs=[row, pl.BlockSpec((rows, C_WIDTH), lambda c: (c, 1)), row, row, gate, gate,
                  pl.BlockSpec((rows, 4 * C_HEADS), lambda c: (c, 0))],
        out_specs=row,
        out_shape=jax.ShapeDtypeStruct((S, C_WIDTH), BF),
        scratch_shapes=[pltpu.VMEM((C_HEADS, C_HEAD_DIM, C_HEAD_DIM), F32), pltpu.VMEM((C_HEADS, 1, C_HEAD_DIM), F32)],
        compiler_params=_params(("arbitrary",)),
        name="mlstm",
    )(qk, qk, v, og, r_rows, dec_rows, colfeat)


def _merge_kernel(h_ref, wga_ref, wgb_ref, wgc_ref, ya_ref, yb_ref, yc_ref, pa_ref, pb_ref, pc_ref, o_ref):
    h = h_ref[...]

    def branch(wg_ref, y_ref, p_ref):
        return _sigmoid(_dot(h, wg_ref[...])) * _dot(y_ref[...], p_ref[...])

    merged = branch(wga_ref, ya_ref, pa_ref) + branch(wgb_ref, yb_ref, pb_ref) + branch(wgc_ref, yc_ref, pc_ref)
    o_ref[...] = merged.astype(o_ref.dtype)


def _merge(h, wg, layer, ya, yb, yc, pa, pb, pc, tm):
    S, D = h.shape
    tn = 512
    nt = D // tn

    def gspec(b):
        return pl.BlockSpec((None, D, tn), lambda m, n: (layer, 0, b * nt + n))

    def yspec(w):
        return pl.BlockSpec((tm, w), lambda m, n: (m, 0))

    def pspec(w):
        return pl.BlockSpec((None, w, tn), lambda m, n: (layer, 0, n))

    return pl.pallas_call(
        _merge_kernel,
        grid=(S // tm, nt),
        in_specs=[pl.BlockSpec((tm, D), lambda m, n: (m, 0)), gspec(0), gspec(1), gspec(2),
                  yspec(A_WIDTH), yspec(B_WIDTH), yspec(C_WIDTH), pspec(A_WIDTH), pspec(B_WIDTH), pspec(C_WIDTH)],
        out_specs=pl.BlockSpec((tm, tn), lambda m, n: (m, n)),
        out_shape=jax.ShapeDtypeStruct((S, D), BF),
        compiler_params=_params(("parallel", "arbitrary")),
        name="merge",
    )(h, wg, wg, wg, ya, yb, yc, pa, pb, pc)


def _outproj_kernel(mg_ref, w_ref, x_ref, g_ref, xo_ref, ho_ref):
    xn = x_ref[...] + _dot(mg_ref[...], w_ref[...])
    xo_ref[...] = xn
    ho_ref[...] = _rms(xn, g_ref[...]).astype(ho_ref.dtype)


def _outproj(mg, wout, layer, x, g2, tm):
    S, D = x.shape
    return pl.pallas_call(
        _outproj_kernel,
        grid=(S // tm,),
        in_specs=[
            pl.BlockSpec((tm, D), lambda m: (m, 0)),
            _resident((None, D, D), lambda m: (layer, 0, 0)),
            pl.BlockSpec((tm, D), lambda m: (m, 0)),
            pl.BlockSpec((None, 1, D), lambda m: (layer, 0, 0)),
        ],
        out_specs=[pl.BlockSpec((tm, D), lambda m: (m, 0)), pl.BlockSpec((tm, D), lambda m: (m, 0))],
        out_shape=[jax.ShapeDtypeStruct((S, D), F32), jax.ShapeDtypeStruct((S, D), BF)],
        compiler_params=_params(("parallel",)),
        name="outproj",
    )(mg, wout, x, g2)


def _ffn_up_kernel(h_ref, wg_ref, wu_ref, cwg_ref, cwu_ref, cbg_ref, cbu_ref, o_ref, ext_ref, halo_ref, *, tm):
    m = pl.program_id(0)
    n = pl.program_id(1)
    h = h_ref[...]
    outs = []
    for i, (w_ref, cw_ref, cb_ref) in enumerate(((wg_ref, cwg_ref, cbg_ref), (wu_ref, cwu_ref, cbu_ref))):
        acc = _dot(h, w_ref[...])
        halo = jnp.where(m > 0, halo_ref[n, i], 0.0)
        outs.append(_causal_conv(ext_ref.at[i], acc, halo, cw_ref[...], cb_ref[...], FFN_CONV, tm))
        halo_ref[n, i] = acc[tm - HALO:, :]
    a_gate, a_up = outs
    o_ref[...] = (a_gate * _sigmoid(a_gate) * a_up).astype(o_ref.dtype)


def _ffn_up(h, wup, layer, cw, cb, tm):
    S, D = h.shape
    tn = 512
    nt = D_FF // tn

    def wspec(off):
        return pl.BlockSpec((None, D, tn), lambda m, n: (layer, 0, off + n))

    def cspec(rows, off):
        return pl.BlockSpec((None, rows, tn), lambda m, n: (layer, 0, off + n))

    return pl.pallas_call(
        functools.partial(_ffn_up_kernel, tm=tm),
        grid=(S // tm, nt),
        in_specs=[pl.BlockSpec((tm, D), lambda m, n: (m, 0)), wspec(0), wspec(nt),
                  cspec(FFN_CONV, 0), cspec(FFN_CONV, nt), cspec(1, 0), cspec(1, nt)],
        out_specs=pl.BlockSpec((tm, tn), lambda m, n: (m, n)),
        out_shape=jax.ShapeDtypeStruct((S, D_FF), BF),
        scratch_shapes=[pltpu.VMEM((2, tm + HALO, tn), F32), pltpu.VMEM((nt, 2, HALO, tn), F32)],
        compiler_params=_params(("arbitrary", "arbitrary")),
        name="ffn_up",
    )(h, wup, wup, cw, cw, cb, cb)


def _ffn_down_kernel(a_ref, w_ref, x_ref, g_ref, *out_refs, last):
    xn = x_ref[...] + _dot(a_ref[...], w_ref[...])
    if last:
        out_refs[0][...] = _rms(xn, g_ref[...])
    else:
        out_refs[0][...] = xn
        out_refs[1][...] = _rms(xn, g_ref[...]).astype(out_refs[1].dtype)


def _ffn_down(a, wdown, layer, x, g, last, tm):
    S, D = x.shape
    row = pl.BlockSpec((tm, D), lambda m: (m, 0))
    if last:
        out_specs, out_shape = row, jax.ShapeDtypeStruct((S, D), F32)
    else:
        out_specs = [row, row]
        out_shape = [jax.ShapeDtypeStruct((S, D), F32), jax.ShapeDtypeStruct((S, D), BF)]
    return pl.pallas_call(
        functools.partial(_ffn_down_kernel, last=last),
        grid=(S // tm,),
        in_specs=[
            pl.BlockSpec((tm, D_FF), lambda m: (m, 0)),
            _resident((None, D_FF, D), lambda m: (layer, 0, 0)),
            row,
            pl.BlockSpec((1, D), lambda m: (0, 0)),
        ],
        out_specs=out_specs,
        out_shape=out_shape,
        compiler_params=_params(("parallel",)),
        name="ffn_down",
    )(a, wdown, x, g)


def _rope_lane_tables(S):
    half = ROT_DIM // 2
    inv = jnp.power(jnp.float32(ROPE_THETA), -jnp.arange(half, dtype=F32) * (2.0 / ROT_DIM))
    ang = jnp.arange(S, dtype=F32)[:, None] * inv[None, :]
    cos, sin = jnp.cos(ang), jnp.sin(ang)
    zeros = jnp.zeros((S, HEAD_DIM - ROT_DIM), F32)
    zh = jnp.zeros((S, half), F32)
    cosf = jnp.concatenate([cos, cos, jnp.ones((S, HEAD_DIM - ROT_DIM), F32)], axis=1)
    sina = jnp.concatenate([-sin, zh, zeros], axis=1)
    sinb = jnp.concatenate([zh, sin, zeros], axis=1)
    return cosf, sina, sinb


def kernel(x, norm1_g, w_in, conv_qk_w, conv_qk_b, b_igate, b_fgate, sgu_norm_g, sgu_w, sgu_b, w_branch_a, w_branch_b, w_branch_c, w_out, norm2_g, w_up, ffn_conv_w, ffn_conv_b, w_down, final_norm_g):
    B, S, D = x.shape
    depth = w_in.shape[0]
    assert B == 1 and D == D_MODEL and S % DIL_GROUPS[-1][0] == 0
    nc = S // C_CHUNK
    tm = min(1024, S)
    tm_row = min(512, S)
    tm_down = min(256, S)

    o_gate = A_COLS + B_COLS + C_MAIN
    o_g = o_gate + C_GATES
    w_bf = _cast_leading_cols(w_in, o_gate + LANE, 256)
    wg = _cast_trailing_cols(w_in, o_g, G_COLS)
    gate_bias = jnp.pad(jnp.concatenate([b_igate, b_fgate], axis=1), ((0, 0), (0, LANE - C_GATES)))[:, None, :].astype(F32)
    pa, pb, pc = w_branch_a.astype(BF), w_branch_b.astype(BF), w_branch_c.astype(BF)
    wout, wup, wdown = w_out.astype(BF), w_up.astype(BF), w_down.astype(BF)
    sgu_b_b = jnp.broadcast_to(sgu_b[:, :, :, None], sgu_b.shape + (SGU_GROUP_DIM,)).astype(F32)
    cosf, sina, sinb = _rope_lane_tables(S)

    xs = x[0]
    h = _rmsnorm(xs, norm1_g[0][None, :], tm_row)
    for l in range(depth):
        chain = None
        for g in reversed(range(N_DIL)):
            a_g = _inproj_a(h, w_bf, l, g, cosf, sina, sinb, tm)
            chain = _attn_group(a_g, g, S, chain)
        y_a = chain
        y_b = _sgu(h, w_bf, l, sgu_norm_g[l][None, :], sgu_w[l], sgu_b_b[l], tm_row)
        qk_c, v_c, og_c, gts = _inproj_c(h, w_bf, l, conv_qk_w, conv_qk_b[:, None, :], gate_bias, tm)
        gate_rows = gts[:, :C_GATES].T.reshape(C_GATES, nc, C_CHUNK)
        r, u, inter, eneg, wk, dec = _mlstm_prep(gate_rows, nc)
        colfeat = jnp.stack([u, inter, eneg, wk], axis=1).transpose(2, 3, 0, 1).reshape(S, 4 * C_HEADS)
        y_c = _mlstm(qk_c, v_c, og_c, r.transpose(1, 0, 2), dec.transpose(1, 0, 2), colfeat, S, C_SUB)
        mg = _merge(h, wg, l, y_a, y_b, y_c, pa, pb, pc, tm)
        xs, h2 = _outproj(mg, wout, l, xs, norm2_g[:, None, :], tm_row)
        act = _ffn_up(h2, wup, l, ffn_conv_w, ffn_conv_b[:, None, :], tm)
        if l + 1 < depth:
            xs, h = _ffn_down(act, wdown, l, xs, norm1_g[l + 1][None, :], False, tm_down)
        else:
            xs = _ffn_down(act, wdown, l, xs, final_norm_g[None, :], True, tm_down)
    return xs[None]
```

```python
import functools
import math

import jax
import jax.numpy as jnp
from jax import lax
from jax.experimental import pallas as pl
from jax.experimental.pallas import tpu as pltpu

BF = jnp.bfloat16
F32 = jnp.float32

D_MODEL = 2048
HEAD_DIM = 128
ROT_DIM = HEAD_DIM // 4
ROPE_THETA = 500000.0
DIL_GROUPS = ((128, 1), (512, 4), (2048, 16))
N_DIL = len(DIL_GROUPS)
A_HEADS = 4
A_WIDTH = A_HEADS * HEAD_DIM
ATT_BLK = 128
ATT_TILE = 1024
SGU_CHUNK = 128
SGU_GROUPS = 4
SGU_GROUP_DIM = 128
B_WIDTH = SGU_GROUPS * SGU_GROUP_DIM
C_HEADS = 4
C_HEAD_DIM = 256
C_WIDTH = C_HEADS * C_HEAD_DIM
C_CHUNK = 128
C_SUB = 2
C_CONV = 4
N_BRANCH = 3
D_FF = 5632
FFN_CONV = 3
EPS = 1e-6

A_COLS = 3 * N_DIL * A_WIDTH
B_COLS = 2 * B_WIDTH
C_MAIN = 4 * C_WIDTH
C_GATES = 2 * C_HEADS
G_COLS = N_BRANCH * D_MODEL

LANE = 128
LOG2E = math.log2(math.e)
HALO = 8
VMEM_LIMIT = 56 * 1024 * 1024


def _params(sem):
    return pltpu.CompilerParams(dimension_semantics=sem, vmem_limit_bytes=VMEM_LIMIT)


def _resident(block_shape, index_map):
    return pl.BlockSpec(block_shape, index_map, pipeline_mode=pl.Buffered(1))


def _sigmoid(x):
    return 1.0 / (1.0 + jnp.exp(-x))


def _rms(x, g):
    return x * lax.rsqrt(jnp.mean(x * x, axis=-1, keepdims=True) + EPS) * g


def _dot(a, b):
    return jnp.dot(a, b, preferred_element_type=F32)


def _att_tile(d):
    return max(ATT_TILE, ATT_BLK * d)


def _cast_t_kernel(*refs, skip, tn):
    o_ref = refs[-1]
    x = refs[0][...]
    if skip:
        x = jnp.concatenate([x, refs[1][...]], axis=0)[skip:skip + tn, :]
    o_ref[...] = x.T.astype(o_ref.dtype)


def _cast_t(wt, start, ncols, tn):
    L, _, D = wt.shape
    skip = start % tn
    blk0 = start // tn
    assert skip % HALO == 0 and ncols % tn == 0
    in_specs = [pl.BlockSpec((None, tn, D), lambda l, n: (l, blk0 + n, 0))]
    args = [wt]
    if skip:
        in_specs.append(pl.BlockSpec((None, skip, D), lambda l, n: (l, (blk0 + n + 1) * (tn // skip), 0)))
        args.append(wt)
    return pl.pallas_call(
        functools.partial(_cast_t_kernel, skip=skip, tn=tn),
        grid=(L, ncols // tn),
        in_specs=in_specs,
        out_specs=pl.BlockSpec((None, D, tn), lambda l, n: (l, 0, n)),
        out_shape=jax.ShapeDtypeStruct((L, D, ncols), BF),
        compiler_params=_params(("parallel", "parallel")),
        name="cast_w_in",
    )(*args)


def _rmsnorm_kernel(x_ref, g_ref, o_ref):
    o_ref[...] = _rms(x_ref[...], g_ref[...]).astype(o_ref.dtype)


def _rmsnorm(x, g, tm):
    S, D = x.shape
    return pl.pallas_call(
        _rmsnorm_kernel,
        grid=(S // tm,),
        in_specs=[pl.BlockSpec((tm, D), lambda m: (m, 0)), pl.BlockSpec((1, D), lambda m: (0, 0))],
        out_specs=pl.BlockSpec((tm, D), lambda m: (m, 0)),
        out_shape=jax.ShapeDtypeStruct((S, D), BF),
        compiler_params=_params(("parallel",)),
        name="rmsnorm",
    )(x, g)


def _inproj_a_kernel(h_ref, wq_ref, wk_ref, wv_ref, cos_ref, sa_ref, sb_ref, o_ref, *scratch, d, tm):
    h = h_ref[...]
    c, sa, sb = cos_ref[...], sa_ref[...], sb_ref[...]
    for t, w_ref in enumerate((wq_ref, wk_ref, wv_ref)):
        acc = _dot(h, w_ref[...])
        for j in range(A_HEADS):
            xj = acc[:, j * HEAD_DIM:(j + 1) * HEAD_DIM]
            if t < 2:
                xj = xj * c + pltpu.roll(xj, HEAD_DIM - ROT_DIM // 2, 1) * sa + pltpu.roll(xj, ROT_DIM // 2, 1) * sb
            if t == 0:
                xj = xj * (HEAD_DIM ** -0.5 * LOG2E)
            cols = slice(t * A_WIDTH + j * HEAD_DIM, t * A_WIDTH + (j + 1) * HEAD_DIM)
            if d == 1:
                o_ref[0, :, cols] = xj.astype(o_ref.dtype)
            else:
                scr = scratch[0]
                scr[t, j] = xj
                for r in range(d):
                    o_ref[r, :, cols] = scr[t, j, pl.ds(r, tm // d, stride=d), :].astype(o_ref.dtype)


def _inproj_a(h, wa, layer, g, cosf, sina, sinb, tm):
    S, D = h.shape
    _, d = DIL_GROUPS[g]
    T = _att_tile(d)
    per = T // tm

    def wspec(t):
        return pl.BlockSpec((None, D, A_WIDTH), lambda m: (layer, 0, t * N_DIL + g))

    tab = pl.BlockSpec((tm, HEAD_DIM), lambda m: (m, 0))
    return pl.pallas_call(
        functools.partial(_inproj_a_kernel, d=d, tm=tm),
        grid=(S // tm,),
        in_specs=[pl.BlockSpec((tm, D), lambda m: (m, 0)), wspec(0), wspec(1), wspec(2), tab, tab, tab],
        out_specs=pl.BlockSpec((None, d, tm // d, 3 * A_WIDTH), lambda m: (m // per, 0, m % per, 0)),
        out_shape=jax.ShapeDtypeStruct((S // T, d, T // d, 3 * A_WIDTH), BF),
        scratch_shapes=[] if d == 1 else [pltpu.VMEM((3, A_HEADS, tm, HEAD_DIM), F32)],
        compiler_params=_params(("parallel",)),
        name=f"inproj_a_g{g}",
    )(h, wa, wa, wa, cosf, sina, sinb)


def _attn_kernel(*refs, d, tile, chain_in, final):
    qc_ref, kc_ref, vc_ref, kp_ref, vp_ref = refs[:5]
    pos = 5
    if chain_in:
        oin_ref, lin_ref = refs[pos:pos + 2]
        pos += 2
    if final:
        y_ref = refs[pos]
        pos += 1
    else:
        o_ref, l_ref = refs[pos:pos + 2]
        pos += 2
    kfull, vfull, bias_ref = refs[pos:pos + 3]

    n = pl.program_id(0)
    kfull[:, 0:ATT_BLK, :] = kp_ref[...]
    kfull[:, ATT_BLK:, :] = kc_ref[...]
    vfull[:, 0:ATT_BLK, :] = vp_ref[...]
    vfull[:, ATT_BLK:, :] = vc_ref[...]
    i = lax.broadcasted_iota(jnp.int32, (ATT_BLK, 2 * ATT_BLK), 0)
    kk = lax.broadcasted_iota(jnp.int32, (ATT_BLK, 2 * ATT_BLK), 1)
    band = (kk >= i) & (kk <= i + ATT_BLK)
    bias_ref[0] = jnp.where(band, 0.0, -jnp.inf)
    bias_ref[1] = jnp.where(band & (kk >= ATT_BLK), 0.0, -jnp.inf)

    def pair(p, carry):
        r = p % d
        b = p // d
        first = jnp.where((n == 0) & (b == 0), 1, 0)
        row0 = pl.multiple_of(b * ATT_BLK, ATT_BLK)
        start = b * (ATT_BLK * d) + r
        for hd in range(A_HEADS):
            sl = slice(hd * HEAD_DIM, (hd + 1) * HEAD_DIM)
            q = qc_ref[r, pl.ds(row0, ATT_BLK), sl]
            k = kfull[r, pl.ds(row0, 2 * ATT_BLK), sl]
            v = vfull[r, pl.ds(row0, 2 * ATT_BLK), sl]
            s = lax.dot_general(q, k, (((1,), (1,)), ((), ())), preferred_element_type=F32) + bias_ref[first]
            m = jnp.max(s, axis=-1, keepdims=True)
            e = jnp.exp2(s - m)
            l = jnp.sum(e, axis=-1, keepdims=True)
            o = _dot(e.astype(BF), v) / l
            lse = jnp.broadcast_to(m + jnp.log2(l), (ATT_BLK, HEAD_DIM))
            rows = pl.ds(start, ATT_BLK, stride=d) if d > 1 else pl.ds(pl.multiple_of(start, ATT_BLK), ATT_BLK)
            if chain_in:
                o2 = oin_ref[hd, rows, :]
                l2 = lin_ref[hd, rows, :]
                mx = jnp.maximum(lse, l2)
                w1, w2 = jnp.exp2(lse - mx), jnp.exp2(l2 - mx)
                den = w1 + w2
                o = (w1 * o + w2 * o2) / den
                lse = mx + jnp.log2(den)
            if final:
                y_ref[rows, sl] = o.astype(y_ref.dtype)
            else:
                o_ref[hd, rows, :] = o
                l_ref[hd, rows, :] = lse
        return carry

    lax.fori_loop(0, tile // ATT_BLK, pair, 0, unroll=True)


def _attn_group(a_g, g, S, chain):
    _, d = DIL_GROUPS[g]
    T = _att_tile(d)
    J = T // d
    final = g == 0
    assert not (final and d != 1)

    def cur(c):
        return pl.BlockSpec((None, d, J, A_WIDTH), lambda n: (n, 0, 0, c))

    def prev(c):
        return pl.BlockSpec((None, d, ATT_BLK, A_WIDTH), lambda n: (jnp.maximum(n - 1, 0), 0, J // ATT_BLK - 1, c))

    hm = pl.BlockSpec((A_HEADS, T, HEAD_DIM), lambda n: (0, n, 0))
    in_specs = [cur(0), cur(1), cur(2), prev(1), prev(2)]
    args = [a_g] * 5
    if chain is not None:
        in_specs += [hm, hm]
        args += list(chain)
    if final:
        out_specs = pl.BlockSpec((T, A_WIDTH), lambda n: (n, 0))
        out_shape = jax.ShapeDtypeStruct((S, A_WIDTH), BF)
    else:
        out_specs = [hm, hm]
        out_shape = [jax.ShapeDtypeStruct((A_HEADS, S, HEAD_DIM), F32)] * 2
    return pl.pallas_call(
        functools.partial(_attn_kernel, d=d, tile=T, chain_in=chain is not None, final=final),
        grid=(S // T,),
        in_specs=in_specs,
        out_specs=out_specs,
        out_shape=out_shape,
        scratch_shapes=[pltpu.VMEM((d, ATT_BLK + J, A_WIDTH), BF)] * 2 + [pltpu.VMEM((2, ATT_BLK, 2 * ATT_BLK), F32)],
        compiler_params=_params(("arbitrary",)),
        name=f"attn_g{g}",
    )(*args)


def _gelu_tanh(x):
    return x * (0.5 * (1.0 + jnp.tanh(math.sqrt(2.0 / math.pi) * (x + 0.044715 * (x * x * x)))))


def _sgu_kernel(h_ref, wu_ref, wv_ref, gn_ref, ws_ref, bs_ref, o_ref, *, tm):
    h = h_ref[...]
    u = _gelu_tanh(_dot(h, wu_ref[...]))
    v = _rms(_gelu_tanh(_dot(h, wv_ref[...])), gn_ref[...]).astype(BF)
    ti = lax.broadcasted_iota(jnp.int32, (SGU_CHUNK, SGU_CHUNK), 0)
    si = lax.broadcasted_iota(jnp.int32, (SGU_CHUNK, SGU_CHUNK), 1)
    for g in range(SGU_GROUPS):
        w = jnp.where(si <= ti, ws_ref[g], 0.0).astype(BF)
        cs = slice(g * SGU_GROUP_DIM, (g + 1) * SGU_GROUP_DIM)
        for c in range(tm // SGU_CHUNK):
            rs = slice(c * SGU_CHUNK, (c + 1) * SGU_CHUNK)
            mixed = _dot(w, v[rs, cs]) + bs_ref[g]
            o_ref[rs, cs] = (u[rs, cs] * mixed).astype(o_ref.dtype)


def _sgu(h, w_bf, layer, gn, ws, bs_b, tm):
    S, D = h.shape
    ub = A_COLS // B_WIDTH
    return pl.pallas_call(
        functools.partial(_sgu_kernel, tm=tm),
        grid=(S // tm,),
        in_specs=[
            pl.BlockSpec((tm, D), lambda m: (m, 0)),
            pl.BlockSpec((None, D, B_WIDTH), lambda m: (layer, 0, ub)),
            pl.BlockSpec((None, D, B_WIDTH), lambda m: (layer, 0, ub + 1)),
            pl.BlockSpec((1, B_WIDTH), lambda m: (0, 0)),
            pl.BlockSpec((SGU_GROUPS, SGU_CHUNK, SGU_CHUNK), lambda m: (0, 0, 0)),
            pl.BlockSpec((SGU_GROUPS, SGU_CHUNK, SGU_GROUP_DIM), lambda m: (0, 0, 0)),
        ],
        out_specs=pl.BlockSpec((tm, B_WIDTH), lambda m: (m, 0)),
        out_shape=jax.ShapeDtypeStruct((S, B_WIDTH), BF),
        compiler_params=_params(("parallel",)),
        name="inproj_b_sgu",
    )(h, w_bf, w_bf, gn, ws, bs_b)


def _causal_conv(ext_ref, acc, halo, w, b, taps, tm):
    ext_ref[0:HALO, :] = halo
    ext_ref[HALO:HALO + tm, :] = acc
    y = b + w[taps - 1:taps, :] * acc
    for j in range(taps - 1):
        y = y + w[j:j + 1, :] * ext_ref[HALO + j - taps + 1:HALO + j - taps + 1 + tm, :]
    return y


def _inproj_c_kernel(h_ref, wa_ref, wb_ref, wv_ref, wo_ref, wgt_ref, cw_ref, cb_ref, gb_ref,
                     qk_ref, v_ref, og_ref, gt_ref, ext_ref, halo_ref, *, tm, tn):
    m = pl.program_id(0)
    j = pl.program_id(1)
    h = h_ref[...]

    @pl.when(j == 0)
    def _():
        gt_ref[...] = _dot(h, wgt_ref[...]) + gb_ref[...]

    for i, w_ref in enumerate((wa_ref, wb_ref)):
        cs = slice(i * tn, (i + 1) * tn)
        acc = _dot(h, w_ref[...])
        halo = jnp.where(m > 0, halo_ref[j, i], 0.0)
        y = _causal_conv(ext_ref.at[i], acc, halo, cw_ref[:, cs], cb_ref[:, cs], C_CONV, tm)
        halo_ref[j, i] = acc[tm - HALO:, :]
        qk_ref[:, cs] = (y * _sigmoid(y)).astype(qk_ref.dtype)
    v_ref[...] = _dot(h, wv_ref[...]).astype(v_ref.dtype)
    og_ref[...] = _sigmoid(_dot(h, wo_ref[...])).astype(og_ref.dtype)


def _inproj_c(h, w_bf, wgate, layer, cw, cb, gate_bias, tm):
    S, D = h.shape
    tn = 512
    nj = C_WIDTH // tn
    c0 = (A_COLS + B_COLS) // tn
    qk_slabs = 2 * C_WIDTH // tn

    def wspec(fn):
        return pl.BlockSpec((None, D, tn), lambda m, j: (layer, 0, c0 + fn(j)))

    return pl.pallas_call(
        functools.partial(_inproj_c_kernel, tm=tm, tn=tn),
        grid=(S // tm, nj),
        in_specs=[
            pl.BlockSpec((tm, D), lambda m, j: (m, 0)),
            wspec(lambda j: 2 * j), wspec(lambda j: 2 * j + 1),
            wspec(lambda j: qk_slabs + j), wspec(lambda j: qk_slabs + nj + j),
            pl.BlockSpec((None, D, LANE), lambda m, j: (layer, 0, 0)),
            pl.BlockSpec((None, C_CONV, 2 * tn), lambda m, j: (layer, 0, j)),
            pl.BlockSpec((None, 1, 2 * tn), lambda m, j: (layer, 0, j)),
            pl.BlockSpec((None, 1, LANE), lambda m, j: (layer, 0, 0)),
        ],
        out_specs=[pl.BlockSpec((tm, 2 * tn), lambda m, j: (m, j)),
                   pl.BlockSpec((tm, tn), lambda m, j: (m, j)),
                   pl.BlockSpec((tm, tn), lambda m, j: (m, j)),
                   pl.BlockSpec((tm, LANE), lambda m, j: (m, 0))],
        out_shape=[jax.ShapeDtypeStruct((S, 2 * C_WIDTH), BF),
                   jax.ShapeDtypeStruct((S, C_WIDTH), BF),
                   jax.ShapeDtypeStruct((S, C_WIDTH), BF),
                   jax.ShapeDtypeStruct((S, LANE), F32)],
        scratch_shapes=[pltpu.VMEM((2, tm + HALO, tn), F32), pltpu.VMEM((nj, 2, HALO, tn), F32)],
        compiler_params=_params(("arbitrary", "arbitrary")),
        name="inproj_c",
    )(h, w_bf, w_bf, w_bf, w_bf, wgate, cw, cb, gate_bias)


def _lane_scan(x, op, lane):
    sh = 1
    while sh < C_CHUNK:
        x = jnp.where(lane >= sh, op(x, pltpu.roll(x, sh, x.ndim - 1)), x)
        sh *= 2
    return x


def _mlstm_prep_kernel(g_ref, r_ref, u_ref, inter_ref, eneg_ref, wk_ref, dec_ref, m_ref, bb_ref, rb_ref, *, nc):
    ig = g_ref[0:C_HEADS]
    fg = g_ref[C_HEADS:2 * C_HEADS]
    lane = lax.broadcasted_iota(jnp.int32, ig.shape, 2)
    lf = -(jnp.maximum(-fg, 0.0) + jnp.log1p(jnp.exp(-jnp.abs(fg))))
    b = _lane_scan(lf, jnp.add, lane)
    r = ig - b
    cm = _lane_scan(r, jnp.maximum, lane)
    shp = ig.shape
    bb_ref[...] = jnp.broadcast_to(b[:, :, C_CHUNK - 1:C_CHUNK], shp)
    rb_ref[...] = jnp.broadcast_to(cm[:, :, C_CHUNK - 1:C_CHUNK], shp)

    def step(c, m):
        m_ref[:, pl.ds(c, 1), :] = m
        return bb_ref[:, pl.ds(c, 1), :] + jnp.maximum(m, rb_ref[:, pl.ds(c, 1), :])

    lax.fori_loop(0, nc, step, jnp.zeros((C_HEADS, 1, C_CHUNK), F32))
    m_in = m_ref[...]
    bl = bb_ref[...]
    m_t = b + jnp.maximum(m_in, cm)
    m_new = jnp.broadcast_to(m_t[:, :, C_CHUNK - 1:C_CHUNK], shp)
    r_ref[...] = r
    u_ref[...] = b - m_t
    inter_ref[...] = jnp.exp(b + m_in - m_t)
    eneg_ref[...] = jnp.exp(-m_t)
    wk_ref[...] = jnp.exp(bl - b + ig - m_new)
    dec_ref[...] = jnp.exp(bl + m_in - m_new)


def _mlstm_prep(gate_rows, nc):
    shp = jax.ShapeDtypeStruct((C_HEADS, nc, C_CHUNK), F32)
    return pl.pallas_call(
        functools.partial(_mlstm_prep_kernel, nc=nc),
        out_shape=[shp] * 6,
        scratch_shapes=[pltpu.VMEM((C_HEADS, nc, C_CHUNK), F32)] * 3,
        compiler_params=pltpu.CompilerParams(vmem_limit_bytes=VMEM_LIMIT),
        name="mlstm_prep",
    )(gate_rows)


def _mlstm_kernel(q_ref, k_ref, v_ref, og_ref, r_ref, dec_ref, cf_ref, y_ref, c_ref, n_ref, *, nsub):
    c = pl.program_id(0)

    @pl.when(c == 0)
    def _():
        c_ref[...] = jnp.zeros_like(c_ref)
        n_ref[...] = jnp.zeros_like(n_ref)

    ti = lax.broadcasted_iota(jnp.int32, (C_CHUNK, C_CHUNK), 0)
    si = lax.broadcasted_iota(jnp.int32, (C_CHUNK, C_CHUNK), 1)
    causal = si <= ti
    kscale = C_HEAD_DIM ** -0.5
    cfs = [cf_ref[j * C_CHUNK:(j + 1) * C_CHUNK, :] for j in range(nsub)]
    for h in range(C_HEADS):
        sl = slice(h * C_HEAD_DIM, (h + 1) * C_HEAD_DIM)
        cmat = c_ref[h]
        nvec = n_ref[h]
        for j in range(nsub):
            rs = slice(j * C_CHUNK, (j + 1) * C_CHUNK)
            q, k, v = q_ref[rs, sl], k_ref[rs, sl], v_ref[rs, sl]
            cf = cfs[j]
            u, inter = cf[:, 4 * h:4 * h + 1], cf[:, 4 * h + 1:4 * h + 2]
            eneg, wk = cf[:, 4 * h + 2:4 * h + 3], cf[:, 4 * h + 3:4 * h + 4]
            r = r_ref[j, h:h + 1, :]
            dec = dec_ref[j, h:h + 1, 0:1]
            s = lax.dot_general(q, k, (((1,), (1,)), ((), ())), preferred_element_type=F32) * kscale
            sc = s * jnp.exp(jnp.where(causal, u + r, -jnp.inf))
            num = _dot(sc.astype(BF), v) + inter * _dot(q, cmat.astype(BF))
            qn = jnp.sum(q.astype(F32) * nvec, axis=-1, keepdims=True)
            den = jnp.sum(sc, axis=-1, keepdims=True) + inter * qn
            hh = num / jnp.maximum(jnp.abs(den), eneg)
            y_ref[rs, sl] = (og_ref[rs, sl].astype(F32) * hh).astype(y_ref.dtype)
            kw = k.astype(F32) * (wk * kscale)
            upd = lax.dot_general(kw.astype(BF), v, (((0,), (0,)), ((), ())), preferred_element_type=F32)
            cmat = dec * cmat + upd
            nvec = dec * nvec + jnp.sum(kw, axis=0, keepdims=True)
        c_ref[h] = cmat
        n_ref[h] = nvec


def _mlstm(qk, v, og, r_rows, dec_rows, colfeat, S, nsub):
    nc = S // C_CHUNK
    rows = nsub * C_CHUNK
    row = pl.BlockSpec((rows, C_WIDTH), lambda c: (c, 0))
    gate = pl.BlockSpec((nsub, C_HEADS, C_CHUNK), lambda c: (c, 0, 0))
    return pl.pallas_call(
        functools.partial(_mlstm_kernel, nsub=nsub),
        grid=(nc // nsub,),
        in_specs=[row, pl.BlockSpec((rows, C_WIDTH), lambda c: (c, 1)), row, row, gate, gate,
                  pl.BlockSpec((rows, 4 * C_HEADS), lambda c: (c, 0))],
        out_specs=row,
        out_shape=jax.ShapeDtypeStruct((S, C_WIDTH), BF),
        scratch_shapes=[pltpu.VMEM((C_HEADS, C_HEAD_DIM, C_HEAD_DIM), F32), pltpu.VMEM((C_HEADS, 1, C_HEAD_DIM), F32)],
        compiler_params=_params(("arbitrary",)),
        name="mlstm",
    )(qk, qk, v, og, r_rows, dec_rows, colfeat)


def _merge_kernel(h_ref, wga_ref, wgb_ref, wgc_ref, ya_ref, yb_ref, yc_ref, pa_ref, pb_ref, pc_ref, o_ref):
    h = h_ref[...]

    def branch(wg_ref, y_ref, p_ref):
        return _sigmoid(_dot(h, wg_ref[...])) * _dot(y_ref[...], p_ref[...])

    merged = branch(wga_ref, ya_ref, pa_ref) + branch(wgb_ref, yb_ref, pb_ref) + branch(wgc_ref, yc_ref, pc_ref)
    o_ref[...] = merged.astype(o_ref.dtype)


def _merge(h, wg, layer, ya, yb, yc, pa, pb, pc, tm):
    S, D = h.shape
    tn = 512
    nt = D // tn

    def gspec(b):
        return pl.BlockSpec((None, D, tn), lambda m, n: (layer, 0, b * nt + n))

    def yspec(w):
        return pl.BlockSpec((tm, w), lambda m, n: (m, 0))

    def pspec(w):
        return pl.BlockSpec((None, w, tn), lambda m, n: (layer, 0, n))

    return pl.pallas_call(
        _merge_kernel,
        grid=(S // tm, nt),
        in_specs=[pl.BlockSpec((tm, D), lambda m, n: (m, 0)), gspec(0), gspec(1), gspec(2),
                  yspec(A_WIDTH), yspec(B_WIDTH), yspec(C_WIDTH), pspec(A_WIDTH), pspec(B_WIDTH), pspec(C_WIDTH)],
        out_specs=pl.BlockSpec((tm, tn), lambda m, n: (m, n)),
        out_shape=jax.ShapeDtypeStruct((S, D), BF),
        compiler_params=_params(("parallel", "arbitrary")),
        name="merge",
    )(h, wg, wg, wg, ya, yb, yc, pa, pb, pc)


def _outproj_kernel(mg_ref, w_ref, x_ref, g_ref, xo_ref, ho_ref):
    xn = x_ref[...] + _dot(mg_ref[...], w_ref[...])
    xo_ref[...] = xn
    ho_ref[...] = _rms(xn, g_ref[...]).astype(ho_ref.dtype)


def _outproj(mg, wout, layer, x, g2, tm):
    S, D = x.shape
    return pl.pallas_call(
        _outproj_kernel,
        grid=(S // tm,),
        in_specs=[
            pl.BlockSpec((tm, D), lambda m: (m, 0)),
            _resident((None, D, D), lambda m: (layer, 0, 0)),
            pl.BlockSpec((tm, D), lambda m: (m, 0)),
            pl.BlockSpec((None, 1, D), lambda m: (layer, 0, 0)),
        ],
        out_specs=[pl.BlockSpec((tm, D), lambda m: (m, 0)), pl.BlockSpec((tm, D), lambda m: (m, 0))],
        out_shape=[jax.ShapeDtypeStruct((S, D), F32), jax.ShapeDtypeStruct((S, D), BF)],
        compiler_params=_params(("parallel",)),
        name="outproj",
    )(mg, wout, x, g2)


def _ffn_up_kernel(h_ref, wg_ref, wu_ref, cwg_ref, cwu_ref, cbg_ref, cbu_ref, o_ref, ext_ref, halo_ref, *, tm):
    m = pl.program_id(0)
    n = pl.program_id(1)
    h = h_ref[...]
    outs = []
    for i, (w_ref, cw_ref, cb_ref) in enumerate(((wg_ref, cwg_ref, cbg_ref), (wu_ref, cwu_ref, cbu_ref))):
        acc = _dot(h, w_ref[...])
        halo = jnp.where(m > 0, halo_ref[n, i], 0.0)
        outs.append(_causal_conv(ext_ref.at[i], acc, halo, cw_ref[...], cb_ref[...], FFN_CONV, tm))
        halo_ref[n, i] = acc[tm - HALO:, :]
    a_gate, a_up = outs
    o_ref[...] = (a_gate * _sigmoid(a_gate) * a_up).astype(o_ref.dtype)


def _ffn_up(h, wup, layer, cw, cb, tm):
    S, D = h.shape
    tn = 512
    nt = D_FF // tn

    def wspec(off):
        return pl.BlockSpec((None, D, tn), lambda m, n: (layer, 0, off + n))

    def cspec(rows, off):
        return pl.BlockSpec((None, rows, tn), lambda m, n: (layer, 0, off + n))

    return pl.pallas_call(
        functools.partial(_ffn_up_kernel, tm=tm),
        grid=(S // tm, nt),
        in_specs=[pl.BlockSpec((tm, D), lambda m, n: (m, 0)), wspec(0), wspec(nt),
                  cspec(FFN_CONV, 0), cspec(FFN_CONV, nt), cspec(1, 0), cspec(1, nt)],
        out_specs=pl.BlockSpec((tm, tn), lambda m, n: (m, n)),
        out_shape=jax.ShapeDtypeStruct((S, D_FF), BF),
        scratch_shapes=[pltpu.VMEM((2, tm + HALO, tn), F32), pltpu.VMEM((nt, 2, HALO, tn), F32)],
        compiler_params=_params(("arbitrary", "arbitrary")),
        name="ffn_up",
    )(h, wup, wup, cw, cw, cb, cb)


def _ffn_down_kernel(a_ref, w_ref, x_ref, g_ref, *out_refs, last):
    xn = x_ref[...] + _dot(a_ref[...], w_ref[...])
    if last:
        out_refs[0][...] = _rms(xn, g_ref[...])
    else:
        out_refs[0][...] = xn
        out_refs[1][...] = _rms(xn, g_ref[...]).astype(out_refs[1].dtype)


def _ffn_down(a, wdown, layer, x, g, last, tm):
    S, D = x.shape
    row = pl.BlockSpec((tm, D), lambda m: (m, 0))
    if last:
        out_specs, out_shape = row, jax.ShapeDtypeStruct((S, D), F32)
    else:
        out_specs = [row, row]
        out_shape = [jax.ShapeDtypeStruct((S, D), F32), jax.ShapeDtypeStruct((S, D), BF)]
    return pl.pallas_call(
        functools.partial(_ffn_down_kernel, last=last),
        grid=(S // tm,),
        in_specs=[
            pl.BlockSpec((tm, D_FF), lambda m: (m, 0)),
            _resident((None, D_FF, D), lambda m: (layer, 0, 0)),
            row,
            pl.BlockSpec((1, D), lambda m: (0, 0)),
        ],
        out_specs=out_specs,
        out_shape=out_shape,
        compiler_params=_params(("parallel",)),
        name="ffn_down",
    )(a, wdown, x, g)


def _rope_lane_tables(S):
    half = ROT_DIM // 2
    inv = jnp.power(jnp.float32(ROPE_THETA), -jnp.arange(half, dtype=F32) * (2.0 / ROT_DIM))
    ang = jnp.arange(S, dtype=F32)[:, None] * inv[None, :]
    cos, sin = jnp.cos(ang), jnp.sin(ang)
    zeros = jnp.zeros((S, HEAD_DIM - ROT_DIM), F32)
    zh = jnp.zeros((S, half), F32)
    cosf = jnp.concatenate([cos, cos, jnp.ones((S, HEAD_DIM - ROT_DIM), F32)], axis=1)
    sina = jnp.concatenate([-sin, zh, zeros], axis=1)
    sinb = jnp.concatenate([zh, sin, zeros], axis=1)
    return cosf, sina, sinb


def kernel(x, norm1_g, w_in, conv_qk_w, conv_qk_b, b_igate, b_fgate, sgu_norm_g, sgu_w, sgu_b, w_branch_a, w_branch_b, w_branch_c, w_out, norm2_g, w_up, ffn_conv_w, ffn_conv_b, w_down, final_norm_g):
    B, S, D = x.shape
    depth = w_in.shape[0]
    assert B == 1 and D == D_MODEL and S % DIL_GROUPS[-1][0] == 0
    nc = S // C_CHUNK
    tm = min(1024, S)
    tm_row = min(512, S)
    tm_down = min(256, S)

    o_gate = A_COLS + B_COLS + C_MAIN
    o_g = o_gate + C_GATES
    wt = jnp.transpose(w_in, (0, 2, 1))
    w_bf = _cast_t(wt, 0, o_gate, 512)
    wgate = _cast_t(wt, o_gate, LANE, LANE)
    wg = _cast_t(wt, o_g, G_COLS, 512)
    gate_bias = jnp.pad(jnp.concatenate([b_igate, b_fgate], axis=1), ((0, 0), (0, LANE - C_GATES)))[:, None, :].astype(F32)
    pa, pb, pc = w_branch_a.astype(BF), w_branch_b.astype(BF), w_branch_c.astype(BF)
    wout, wup, wdown = w_out.astype(BF), w_up.astype(BF), w_down.astype(BF)
    sgu_b_b = jnp.broadcast_to(sgu_b[:, :, :, None], sgu_b.shape + (SGU_GROUP_DIM,)).astype(F32)
    cosf, sina, sinb = _rope_lane_tables(S)

    xs = x[0]
    h = _rmsnorm(xs, norm1_g[0][None, :], tm_row)
    for l in range(depth):
        chain = None
        for g in reversed(range(N_DIL)):
            a_g = _inproj_a(h, w_bf, l, g, cosf, sina, sinb, tm)
            chain = _attn_group(a_g, g, S, chain)
        y_a = chain
        y_b = _sgu(h, w_bf, l, sgu_norm_g[l][None, :], sgu_w[l], sgu_b_b[l], tm_row)
        qk_c, v_c, og_c, gts = _inproj_c(h, w_bf, wgate, l, conv_qk_w, conv_qk_b[:, None, :], gate_bias, tm)
        gate_rows = gts[:, :C_GATES].T.reshape(C_GATES, nc, C_CHUNK)
        r, u, inter, eneg, wk, dec = _mlstm_prep(gate_rows, nc)
        colfeat = jnp.stack([u, inter, eneg, wk], axis=1).transpose(2, 3, 0, 1).reshape(S, 4 * C_HEADS)
        y_c = _mlstm(qk_c, v_c, og_c, r.transpose(1, 0, 2), dec.transpose(1, 0, 2), colfeat, S, C_SUB)
        mg = _merge(h, wg, l, y_a, y_b, y_c, pa, pb, pc, tm)
        xs, h2 = _outproj(mg, wout, l, xs, norm2_g[:, None, :], tm_row)
        act = _ffn_up(h2, wup, l, ffn_conv_w, ffn_conv_b[:, None, :], tm)
        if l + 1 < depth:
            xs, h = _ffn_down(act, wdown, l, xs, norm1_g[l + 1][None, :], False, tm_down)
        else:
            xs = _ffn_down(act, wdown, l, xs, final_norm_g[None, :], True, tm_down)
    return xs[None]
```

```python
import functools
import math

import jax
import jax.numpy as jnp
from jax import lax
from jax.experimental import pallas as pl
from jax.experimental.pallas import tpu as pltpu

BF = jnp.bfloat16
F32 = jnp.float32

D_MODEL = 2048
HEAD_DIM = 128
ROT_DIM = HEAD_DIM // 4
ROPE_THETA = 500000.0
DIL_GROUPS = ((128, 1), (512, 4), (2048, 16))
N_DIL = len(DIL_GROUPS)
A_HEADS = 4
A_WIDTH = A_HEADS * HEAD_DIM
ATT_BLK = 128
ATT_TILE = 1024
SGU_CHUNK = 128
SGU_GROUPS = 4
SGU_GROUP_DIM = 128
B_WIDTH = SGU_GROUPS * SGU_GROUP_DIM
C_HEADS = 4
C_HEAD_DIM = 256
C_WIDTH = C_HEADS * C_HEAD_DIM
C_CHUNK = 128
C_SUB = 2
C_CONV = 4
N_BRANCH = 3
D_FF = 5632
FFN_CONV = 3
EPS = 1e-6

A_COLS = 3 * N_DIL * A_WIDTH
B_COLS = 2 * B_WIDTH
C_MAIN = 4 * C_WIDTH
C_GATES = 2 * C_HEADS
G_COLS = N_BRANCH * D_MODEL

LANE = 128
LOG2E = math.log2(math.e)
HALO = 8
VMEM_LIMIT = 56 * 1024 * 1024


def _params(sem):
    return pltpu.CompilerParams(dimension_semantics=sem, vmem_limit_bytes=VMEM_LIMIT)


def _resident(block_shape, index_map):
    return pl.BlockSpec(block_shape, index_map, pipeline_mode=pl.Buffered(1))


def _sigmoid(x):
    return 1.0 / (1.0 + jnp.exp(-x))


def _rms(x, g):
    return x * lax.rsqrt(jnp.mean(x * x, axis=-1, keepdims=True) + EPS) * g


def _dot(a, b):
    return jnp.dot(a, b, preferred_element_type=F32)


def _att_tile(d):
    return max(ATT_TILE, ATT_BLK * d)


def _cast_t_kernel(*refs, skip, tn):
    o_ref = refs[-1]
    x = refs[0][...]
    if skip:
        x = jnp.concatenate([x, refs[1][...]], axis=0)[skip:skip + tn, :]
    o_ref[...] = x.T.astype(o_ref.dtype)


def _cast_t(wt, start, ncols, tn):
    L, _, D = wt.shape
    skip = start % tn
    blk0 = start // tn
    assert skip % HALO == 0 and ncols % tn == 0
    in_specs = [pl.BlockSpec((None, tn, D), lambda l, n: (l, blk0 + n, 0))]
    args = [wt]
    if skip:
        in_specs.append(pl.BlockSpec((None, skip, D), lambda l, n: (l, (blk0 + n + 1) * (tn // skip), 0)))
        args.append(wt)
    return pl.pallas_call(
        functools.partial(_cast_t_kernel, skip=skip, tn=tn),
        grid=(L, ncols // tn),
        in_specs=in_specs,
        out_specs=pl.BlockSpec((None, D, tn), lambda l, n: (l, 0, n)),
        out_shape=jax.ShapeDtypeStruct((L, D, ncols), BF),
        compiler_params=_params(("parallel", "parallel")),
        name="cast_w_in",
    )(*args)


def _rmsnorm_kernel(x_ref, g_ref, o_ref):
    o_ref[...] = _rms(x_ref[...], g_ref[...]).astype(o_ref.dtype)


def _rmsnorm(x, g, tm):
    S, D = x.shape
    return pl.pallas_call(
        _rmsnorm_kernel,
        grid=(S // tm,),
        in_specs=[pl.BlockSpec((tm, D), lambda m: (m, 0)), pl.BlockSpec((1, D), lambda m: (0, 0))],
        out_specs=pl.BlockSpec((tm, D), lambda m: (m, 0)),
        out_shape=jax.ShapeDtypeStruct((S, D), BF),
        compiler_params=_params(("parallel",)),
        name="rmsnorm",
    )(x, g)


def _inproj_a_kernel(h_ref, wq_ref, wk_ref, wv_ref, cos_ref, sa_ref, sb_ref, o_ref, *scratch, d, tm):
    h = h_ref[...]
    c, sa, sb = cos_ref[...], sa_ref[...], sb_ref[...]
    for t, w_ref in enumerate((wq_ref, wk_ref, wv_ref)):
        acc = _dot(h, w_ref[...])
        for j in range(A_HEADS):
            xj = acc[:, j * HEAD_DIM:(j + 1) * HEAD_DIM]
            if t < 2:
                xj = xj * c + pltpu.roll(xj, HEAD_DIM - ROT_DIM // 2, 1) * sa + pltpu.roll(xj, ROT_DIM // 2, 1) * sb
            if t == 0:
                xj = xj * (HEAD_DIM ** -0.5 * LOG2E)
            cols = slice(t * A_WIDTH + j * HEAD_DIM, t * A_WIDTH + (j + 1) * HEAD_DIM)
            if d == 1:
                o_ref[0, :, cols] = xj.astype(o_ref.dtype)
            else:
                scr = scratch[0]
                scr[t, j] = xj
                for r in range(d):
                    o_ref[r, :, cols] = scr[t, j, pl.ds(r, tm // d, stride=d), :].astype(o_ref.dtype)


def _inproj_a(h, wa, layer, g, cosf, sina, sinb, tm):
    S, D = h.shape
    _, d = DIL_GROUPS[g]
    T = _att_tile(d)
    per = T // tm

    def wspec(t):
        return pl.BlockSpec((None, D, A_WIDTH), lambda m: (layer, 0, t * N_DIL + g))

    tab = pl.BlockSpec((tm, HEAD_DIM), lambda m: (m, 0))
    return pl.pallas_call(
        functools.partial(_inproj_a_kernel, d=d, tm=tm),
        grid=(S // tm,),
        in_specs=[pl.BlockSpec((tm, D), lambda m: (m, 0)), wspec(0), wspec(1), wspec(2), tab, tab, tab],
        out_specs=pl.BlockSpec((None, d, tm // d, 3 * A_WIDTH), lambda m: (m // per, 0, m % per, 0)),
        out_shape=jax.ShapeDtypeStruct((S // T, d, T // d, 3 * A_WIDTH), BF),
        scratch_shapes=[] if d == 1 else [pltpu.VMEM((3, A_HEADS, tm, HEAD_DIM), F32)],
        compiler_params=_params(("parallel",)),
        name=f"inproj_a_g{g}",
    )(h, wa, wa, wa, cosf, sina, sinb)


def _attn_kernel(*refs, d, tile, chain_in, final):
    qc_ref, kc_ref, vc_ref, kp_ref, vp_ref = refs[:5]
    pos = 5
    if chain_in:
        oin_ref, lin_ref = refs[pos:pos + 2]
        pos += 2
    if final:
        y_ref = refs[pos]
        pos += 1
    else:
        o_ref, l_ref = refs[pos:pos + 2]
        pos += 2
    kfull, vfull, bias_ref = refs[pos:pos + 3]

    n = pl.program_id(0)
    kfull[:, 0:ATT_BLK, :] = kp_ref[...]
    kfull[:, ATT_BLK:, :] = kc_ref[...]
    vfull[:, 0:ATT_BLK, :] = vp_ref[...]
    vfull[:, ATT_BLK:, :] = vc_ref[...]
    i = lax.broadcasted_iota(jnp.int32, (ATT_BLK, 2 * ATT_BLK), 0)
    kk = lax.broadcasted_iota(jnp.int32, (ATT_BLK, 2 * ATT_BLK), 1)
    band = (kk >= i) & (kk <= i + ATT_BLK)
    bias_ref[0] = jnp.where(band, 0.0, -jnp.inf)
    bias_ref[1] = jnp.where(band & (kk >= ATT_BLK), 0.0, -jnp.inf)

    def pair(p, carry):
        r = p % d
        b = p // d
        first = jnp.where((n == 0) & (b == 0), 1, 0)
        row0 = pl.multiple_of(b * ATT_BLK, ATT_BLK)
        start = b * (ATT_BLK * d) + r
        for hd in range(A_HEADS):
            sl = slice(hd * HEAD_DIM, (hd + 1) * HEAD_DIM)
            q = qc_ref[r, pl.ds(row0, ATT_BLK), sl]
            k = kfull[r, pl.ds(row0, 2 * ATT_BLK), sl]
            v = vfull[r, pl.ds(row0, 2 * ATT_BLK), sl]
            s = lax.dot_general(q, k, (((1,), (1,)), ((), ())), preferred_element_type=F32) + bias_ref[first]
            m = jnp.max(s, axis=-1, keepdims=True)
            e = jnp.exp2(s - m)
            l = jnp.sum(e, axis=-1, keepdims=True)
            acc = _dot(e.astype(BF), v)
            lse = jnp.broadcast_to(m + jnp.log2(l), (ATT_BLK, HEAD_DIM))
            rows = pl.ds(start, ATT_BLK, stride=d) if d > 1 else pl.ds(pl.multiple_of(start, ATT_BLK), ATT_BLK)
            if chain_in:
                o2 = oin_ref[hd, rows, :]
                l2 = lin_ref[hd, rows, :]
                mx = jnp.maximum(lse, l2)
                a1, w2 = jnp.exp2(m - mx), jnp.exp2(l2 - mx)
                den = a1 * l + w2
                o = (a1 * acc + w2 * o2) / den
                lse = mx + jnp.log2(den)
            else:
                o = acc / l
            if final:
                y_ref[rows, sl] = o.astype(y_ref.dtype)
            else:
                o_ref[hd, rows, :] = o
                l_ref[hd, rows, :] = lse
        return carry

    lax.fori_loop(0, tile // ATT_BLK, pair, 0, unroll=True)


def _attn_group(a_g, g, S, chain):
    _, d = DIL_GROUPS[g]
    T = _att_tile(d)
    J = T // d
    final = g == 0
    assert not (final and d != 1)

    def cur(c):
        return pl.BlockSpec((None, d, J, A_WIDTH), lambda n: (n, 0, 0, c))

    def prev(c):
        return pl.BlockSpec((None, d, ATT_BLK, A_WIDTH), lambda n: (jnp.maximum(n - 1, 0), 0, J // ATT_BLK - 1, c))

    hm = pl.BlockSpec((A_HEADS, T, HEAD_DIM), lambda n: (0, n, 0))
    in_specs = [cur(0), cur(1), cur(2), prev(1), prev(2)]
    args = [a_g] * 5
    if chain is not None:
        in_specs += [hm, hm]
        args += list(chain)
    if final:
        out_specs = pl.BlockSpec((T, A_WIDTH), lambda n: (n, 0))
        out_shape = jax.ShapeDtypeStruct((S, A_WIDTH), BF)
    else:
        out_specs = [hm, hm]
        out_shape = [jax.ShapeDtypeStruct((A_HEADS, S, HEAD_DIM), F32)] * 2
    return pl.pallas_call(
        functools.partial(_attn_kernel, d=d, tile=T, chain_in=chain is not None, final=final),
        grid=(S // T,),
        in_specs=in_specs,
        out_specs=out_specs,
        out_shape=out_shape,
        scratch_shapes=[pltpu.VMEM((d, ATT_BLK + J, A_WIDTH), BF)] * 2 + [pltpu.VMEM((2, ATT_BLK, 2 * ATT_BLK), F32)],
        compiler_params=_params(("arbitrary",)),
        name=f"attn_g{g}",
    )(*args)


def _gelu_tanh(x):
    return x * (0.5 * (1.0 + jnp.tanh(math.sqrt(2.0 / math.pi) * (x + 0.044715 * (x * x * x)))))


def _sgu_kernel(h_ref, wu_ref, wv_ref, gn_ref, ws_ref, bs_ref, o_ref, *, tm):
    h = h_ref[...]
    u = _gelu_tanh(_dot(h, wu_ref[...]))
    v = _rms(_gelu_tanh(_dot(h, wv_ref[...])), gn_ref[...]).astype(BF)
    ti = lax.broadcasted_iota(jnp.int32, (SGU_CHUNK, SGU_CHUNK), 0)
    si = lax.broadcasted_iota(jnp.int32, (SGU_CHUNK, SGU_CHUNK), 1)
    for g in range(SGU_GROUPS):
        w = jnp.where(si <= ti, ws_ref[g], 0.0).astype(BF)
        cs = slice(g * SGU_GROUP_DIM, (g + 1) * SGU_GROUP_DIM)
        for c in range(tm // SGU_CHUNK):
            rs = slice(c * SGU_CHUNK, (c + 1) * SGU_CHUNK)
            mixed = _dot(w, v[rs, cs]) + bs_ref[g]
            o_ref[rs, cs] = (u[rs, cs] * mixed).astype(o_ref.dtype)


def _sgu(h, w_bf, layer, gn, ws, bs_b, tm):
    S, D = h.shape
    ub = A_COLS // B_WIDTH
    return pl.pallas_call(
        functools.partial(_sgu_kernel, tm=tm),
        grid=(S // tm,),
        in_specs=[
            pl.BlockSpec((tm, D), lambda m: (m, 0)),
            pl.BlockSpec((None, D, B_WIDTH), lambda m: (layer, 0, ub)),
            pl.BlockSpec((None, D, B_WIDTH), lambda m: (layer, 0, ub + 1)),
            pl.BlockSpec((1, B_WIDTH), lambda m: (0, 0)),
            pl.BlockSpec((SGU_GROUPS, SGU_CHUNK, SGU_CHUNK), lambda m: (0, 0, 0)),
            pl.BlockSpec((SGU_GROUPS, SGU_CHUNK, SGU_GROUP_DIM), lambda m: (0, 0, 0)),
        ],
        out_specs=pl.BlockSpec((tm, B_WIDTH), lambda m: (m, 0)),
        out_shape=jax.ShapeDtypeStruct((S, B_WIDTH), BF),
        compiler_params=_params(("parallel",)),
        name="inproj_b_sgu",
    )(h, w_bf, w_bf, gn, ws, bs_b)


def _causal_conv(ext_ref, acc, halo, w, b, taps, tm):
    ext_ref[0:HALO, :] = halo
    ext_ref[HALO:HALO + tm, :] = acc
    y = b + w[taps - 1:taps, :] * acc
    for j in range(taps - 1):
        y = y + w[j:j + 1, :] * ext_ref[HALO + j - taps + 1:HALO + j - taps + 1 + tm, :]
    return y


def _causal_conv4(acc, halo, w, b):
    xe = jnp.concatenate([halo, acc], axis=0)
    s1 = pltpu.roll(xe, 1, 0)
    inner = w[1:2, :] * xe + w[0:1, :] * s1
    outer = w[3:4, :] * acc + w[2:3, :] * s1[HALO:, :]
    return b + outer + pltpu.roll(inner, 2, 0)[HALO:, :]


def _inproj_c_kernel(h_ref, wa_ref, wb_ref, wv_ref, wo_ref, wgt_ref, cw_ref, cb_ref, gb_ref,
                     qk_ref, v_ref, og_ref, gt_ref, halo_ref, *, tm, tn):
    m = pl.program_id(0)
    j = pl.program_id(1)
    h = h_ref[...]

    @pl.when(j == 0)
    def _():
        gt_ref[...] = _dot(h, wgt_ref[...]) + gb_ref[...]

    for i, w_ref in enumerate((wa_ref, wb_ref)):
        cs = slice(i * tn, (i + 1) * tn)
        acc = _dot(h, w_ref[...])
        halo = jnp.where(m > 0, halo_ref[j, i], 0.0)
        y = _causal_conv4(acc, halo, cw_ref[:, cs], cb_ref[:, cs])
        halo_ref[j, i] = acc[tm - HALO:, :]
        qk_ref[:, cs] = (y * _sigmoid(y)).astype(qk_ref.dtype)
    v_ref[...] = _dot(h, wv_ref[...]).astype(v_ref.dtype)
    og_ref[...] = _sigmoid(_dot(h, wo_ref[...])).astype(og_ref.dtype)


def _inproj_c(h, w_bf, wgate, layer, cw, cb, gate_bias, tm):
    S, D = h.shape
    assert C_CONV == 4
    tn = 512
    nj = C_WIDTH // tn
    c0 = (A_COLS + B_COLS) // tn
    qk_slabs = 2 * C_WIDTH // tn

    def wspec(fn):
        return pl.BlockSpec((None, D, tn), lambda m, j: (layer, 0, c0 + fn(j)))

    return pl.pallas_call(
        functools.partial(_inproj_c_kernel, tm=tm, tn=tn),
        grid=(S // tm, nj),
        in_specs=[
            pl.BlockSpec((tm, D), lambda m, j: (m, 0)),
            wspec(lambda j: 2 * j), wspec(lambda j: 2 * j + 1),
            wspec(lambda j: qk_slabs + j), wspec(lambda j: qk_slabs + nj + j),
            pl.BlockSpec((None, D, LANE), lambda m, j: (layer, 0, 0)),
            pl.BlockSpec((None, C_CONV, 2 * tn), lambda m, j: (layer, 0, j)),
            pl.BlockSpec((None, 1, 2 * tn), lambda m, j: (layer, 0, j)),
            pl.BlockSpec((None, 1, LANE), lambda m, j: (layer, 0, 0)),
        ],
        out_specs=[pl.BlockSpec((tm, 2 * tn), lambda m, j: (m, j)),
                   pl.BlockSpec((tm, tn), lambda m, j: (m, j)),
                   pl.BlockSpec((tm, tn), lambda m, j: (m, j)),
                   pl.BlockSpec((tm, LANE), lambda m, j: (m, 0))],
        out_shape=[jax.ShapeDtypeStruct((S, 2 * C_WIDTH), BF),
                   jax.ShapeDtypeStruct((S, C_WIDTH), BF),
                   jax.ShapeDtypeStruct((S, C_WIDTH), BF),
                   jax.ShapeDtypeStruct((S, LANE), F32)],
        scratch_shapes=[pltpu.VMEM((nj, 2, HALO, tn), F32)],
        compiler_params=_params(("arbitrary", "arbitrary")),
        name="inproj_c",
    )(h, w_bf, w_bf, w_bf, w_bf, wgate, cw, cb, gate_bias)


def _lane_scan(x, op, lane):
    sh = 1
    while sh < C_CHUNK:
        x = jnp.where(lane >= sh, op(x, pltpu.roll(x, sh, x.ndim - 1)), x)
        sh *= 2
    return x


def _mlstm_prep_kernel(g_ref, r_ref, u_ref, inter_ref, eneg_ref, wk_ref, dec_ref, m_ref, bb_ref, rb_ref, *, nc):
    ig = g_ref[0:C_HEADS]
    fg = g_ref[C_HEADS:2 * C_HEADS]
    lane = lax.broadcasted_iota(jnp.int32, ig.shape, 2)
    lf = -(jnp.maximum(-fg, 0.0) + jnp.log1p(jnp.exp(-jnp.abs(fg))))
    b = _lane_scan(lf, jnp.add, lane)
    r = ig - b
    cm = _lane_scan(r, jnp.maximum, lane)
    shp = ig.shape
    bb_ref[...] = jnp.broadcast_to(b[:, :, C_CHUNK - 1:C_CHUNK], shp)
    rb_ref[...] = jnp.broadcast_to(cm[:, :, C_CHUNK - 1:C_CHUNK], shp)

    def step(c, m):
        m_ref[:, pl.ds(c, 1), :] = m
        return bb_ref[:, pl.ds(c, 1), :] + jnp.maximum(m, rb_ref[:, pl.ds(c, 1), :])

    lax.fori_loop(0, nc, step, jnp.zeros((C_HEADS, 1, C_CHUNK), F32))
    m_in = m_ref[...]
    bl = bb_ref[...]
    m_t = b + jnp.maximum(m_in, cm)
    m_new = jnp.broadcast_to(m_t[:, :, C_CHUNK - 1:C_CHUNK], shp)
    r_ref[...] = r
    u_ref[...] = b - m_t
    inter_ref[...] = jnp.exp(b + m_in - m_t)
    eneg_ref[...] = jnp.exp(-m_t)
    wk_ref[...] = jnp.exp(bl - b + ig - m_new)
    dec_ref[...] = jnp.exp(bl + m_in - m_new)


def _mlstm_prep(gate_rows, nc):
    shp = jax.ShapeDtypeStruct((C_HEADS, nc, C_CHUNK), F32)
    return pl.pallas_call(
        functools.partial(_mlstm_prep_kernel, nc=nc),
        out_shape=[shp] * 6,
        scratch_shapes=[pltpu.VMEM((C_HEADS, nc, C_CHUNK), F32)] * 3,
        compiler_params=pltpu.CompilerParams(vmem_limit_bytes=VMEM_LIMIT),
        name="mlstm_prep",
    )(gate_rows)


def _mlstm_kernel(q_ref, k_ref, v_ref, og_ref, r_ref, dec_ref, cf_ref, y_ref, c_ref, n_ref, *, nsub):
    c = pl.program_id(0)

    @pl.when(c == 0)
    def _():
        c_ref[...] = jnp.zeros_like(c_ref)
        n_ref[...] = jnp.zeros_like(n_ref)

    ti = lax.broadcasted_iota(jnp.int32, (C_CHUNK, C_CHUNK), 0)
    si = lax.broadcasted_iota(jnp.int32, (C_CHUNK, C_CHUNK), 1)
    causal = si <= ti
    kscale = C_HEAD_DIM ** -0.5
    cfs = [cf_ref[j * C_CHUNK:(j + 1) * C_CHUNK, :] for j in range(nsub)]
    for h in range(C_HEADS):
        sl = slice(h * C_HEAD_DIM, (h + 1) * C_HEAD_DIM)
        cmat = c_ref[h]
        nvec = n_ref[h]
        for j in range(nsub):
            rs = slice(j * C_CHUNK, (j + 1) * C_CHUNK)
            q, k, v = q_ref[rs, sl], k_ref[rs, sl], v_ref[rs, sl]
            cf = cfs[j]
            u, inter = cf[:, 4 * h:4 * h + 1], cf[:, 4 * h + 1:4 * h + 2]
            eneg, wk = cf[:, 4 * h + 2:4 * h + 3], cf[:, 4 * h + 3:4 * h + 4]
            r = r_ref[j, h:h + 1, :]
            dec = dec_ref[j, h:h + 1, 0:1]
            s = lax.dot_general(q, k, (((1,), (1,)), ((), ())), preferred_element_type=F32) * kscale
            sc = s * jnp.exp(jnp.where(causal, u + r, -jnp.inf))
            num = _dot(sc.astype(BF), v) + inter * _dot(q, cmat.astype(BF))
            qn = jnp.sum(q.astype(F32) * nvec, axis=-1, keepdims=True)
            den = jnp.sum(sc, axis=-1, keepdims=True) + inter * qn
            hh = num / jnp.maximum(jnp.abs(den), eneg)
            y_ref[rs, sl] = (og_ref[rs, sl].astype(F32) * hh).astype(y_ref.dtype)
            kw = k.astype(F32) * (wk * kscale)
            upd = lax.dot_general(kw.astype(BF), v, (((0,), (0,)), ((), ())), preferred_element_type=F32)
            cmat = dec * cmat + upd
            nvec = dec * nvec + jnp.sum(kw, axis=0, keepdims=True)
        c_ref[h] = cmat
        n_ref[h] = nvec


def _mlstm(qk, v, og, r_rows, dec_rows, colfeat, S, nsub):
    nc = S // C_CHUNK
    rows = nsub * C_CHUNK
    row = pl.BlockSpec((rows, C_WIDTH), lambda c: (c, 0))
    gate = pl.BlockSpec((nsub, C_HEADS, C_CHUNK), lambda c: (c, 0, 0))
    return pl.pallas_call(
        functools.partial(_mlstm_kernel, nsub=nsub),
        grid=(nc // nsub,),
        in_specs=[row, pl.BlockSpec((rows, C_WIDTH), lambda c: (c, 1)), row, row, gate, gate,
                  pl.BlockSpec((rows, 4 * C_HEADS), lambda c: (c, 0))],
        out_specs=row,
        out_shape=jax.ShapeDtypeStruct((S, C_WIDTH), BF),
        scratch_shapes=[pltpu.VMEM((C_HEADS, C_HEAD_DIM, C_HEAD_DIM), F32), pltpu.VMEM((C_HEADS, 1, C_HEAD_DIM), F32)],
        compiler_params=_params(("arbitrary",)),
        name="mlstm",
    )(qk, qk, v, og, r_rows, dec_rows, colfeat)


def _merge_kernel(h_ref, wga_ref, wgb_ref, wgc_ref, ya_ref, yb_ref, yc_ref, pa_ref, pb_ref, pc_ref, o_ref):
    h = h_ref[...]

    def branch(wg_ref, y_ref, p_ref):
        return _sigmoid(_dot(h, wg_ref[...])) * _dot(y_ref[...], p_ref[...])

    merged = branch(wga_ref, ya_ref, pa_ref) + branch(wgb_ref, yb_ref, pb_ref) + branch(wgc_ref, yc_ref, pc_ref)
    o_ref[...] = merged.astype(o_ref.dtype)


def _merge(h, wg, layer, ya, yb, yc, pa, pb, pc, tm):
    S, D = h.shape
    tn = 512
    nt = D // tn

    def gspec(b):
        return pl.BlockSpec((None, D, tn), lambda m, n: (layer, 0, b * nt + n))

    def yspec(w):
        return pl.BlockSpec((tm, w), lambda m, n: (m, 0))

    def pspec(w):
        return pl.BlockSpec((None, w, tn), lambda m, n: (layer, 0, n))

    return pl.pallas_call(
        _merge_kernel,
        grid=(S // tm, nt),
        in_specs=[pl.BlockSpec((tm, D), lambda m, n: (m, 0)), gspec(0), gspec(1), gspec(2),
                  yspec(A_WIDTH), yspec(B_WIDTH), yspec(C_WIDTH), pspec(A_WIDTH), pspec(B_WIDTH), pspec(C_WIDTH)],
        out_specs=pl.BlockSpec((tm, tn), lambda m, n: (m, n)),
        out_shape=jax.ShapeDtypeStruct((S, D), BF),
        compiler_params=_params(("parallel", "arbitrary")),
        name="merge",
    )(h, wg, wg, wg, ya, yb, yc, pa, pb, pc)


def _outproj_kernel(mg_ref, w_ref, x_ref, g_ref, xo_ref, ho_ref, wb_ref):
    @pl.when(pl.program_id(0) == 0)
    def _():
        wb_ref[...] = w_ref[...].astype(wb_ref.dtype)

    xn = x_ref[...] + _dot(mg_ref[...], wb_ref[...])
    xo_ref[...] = xn
    ho_ref[...] = _rms(xn, g_ref[...]).astype(ho_ref.dtype)


def _outproj(mg, w_out, layer, x, g2, tm):
    S, D = x.shape
    return pl.pallas_call(
        _outproj_kernel,
        grid=(S // tm,),
        in_specs=[
            pl.BlockSpec((tm, D), lambda m: (m, 0)),
            _resident((None, D, D), lambda m: (layer, 0, 0)),
            pl.BlockSpec((tm, D), lambda m: (m, 0)),
            pl.BlockSpec((None, 1, D), lambda m: (layer, 0, 0)),
        ],
        out_specs=[pl.BlockSpec((tm, D), lambda m: (m, 0)), pl.BlockSpec((tm, D), lambda m: (m, 0))],
        out_shape=[jax.ShapeDtypeStruct((S, D), F32), jax.ShapeDtypeStruct((S, D), BF)],
        scratch_shapes=[pltpu.VMEM((D, D), BF)],
        compiler_params=_params(("arbitrary",)),
        name="outproj",
    )(mg, w_out, x, g2)


def _ffn_up_kernel(h_ref, wg_ref, wu_ref, cwg_ref, cwu_ref, cbg_ref, cbu_ref, o_ref, ext_ref, halo_ref, *, tm):
    m = pl.program_id(0)
    n = pl.program_id(1)
    h = h_ref[...]
    outs = []
    for i, (w_ref, cw_ref, cb_ref) in enumerate(((wg_ref, cwg_ref, cbg_ref), (wu_ref, cwu_ref, cbu_ref))):
        acc = _dot(h, w_ref[...])
        halo = jnp.where(m > 0, halo_ref[n, i], 0.0)
        outs.append(_causal_conv(ext_ref.at[i], acc, halo, cw_ref[...], cb_ref[...], FFN_CONV, tm))
        halo_ref[n, i] = acc[tm - HALO:, :]
    a_gate, a_up = outs
    o_ref[...] = (a_gate * _sigmoid(a_gate) * a_up).astype(o_ref.dtype)


def _ffn_up(h, wup, layer, cw, cb, tm):
    S, D = h.shape
    tn = 512
    nt = D_FF // tn

    def wspec(off):
        return pl.BlockSpec((None, D, tn), lambda m, n: (layer, 0, off + n))

    def cspec(rows, off):
        return pl.BlockSpec((None, rows, tn), lambda m, n: (layer, 0, off + n))

    return pl.pallas_call(
        functools.partial(_ffn_up_kernel, tm=tm),
        grid=(S // tm, nt),
        in_specs=[pl.BlockSpec((tm, D), lambda m, n: (m, 0)), wspec(0), wspec(nt),
                  cspec(FFN_CONV, 0), cspec(FFN_CONV, nt), cspec(1, 0), cspec(1, nt)],
        out_specs=pl.BlockSpec((tm, tn), lambda m, n: (m, n)),
        out_shape=jax.ShapeDtypeStruct((S, D_FF), BF),
        scratch_shapes=[pltpu.VMEM((2, tm + HALO, tn), F32), pltpu.VMEM((nt, 2, HALO, tn), F32)],
        compiler_params=_params(("arbitrary", "arbitrary")),
        name="ffn_up",
    )(h, wup, wup, cw, cw, cb, cb)


def _ffn_down_kernel(a_ref, w_ref, x_ref, g_ref, *out_refs, last):
    xn = x_ref[...] + _dot(a_ref[...], w_ref[...])
    if last:
        out_refs[0][...] = _rms(xn, g_ref[...])
    else:
        out_refs[0][...] = xn
        out_refs[1][...] = _rms(xn, g_ref[...]).astype(out_refs[1].dtype)


def _ffn_down(a, wdown, layer, x, g, last, tm):
    S, D = x.shape
    row = pl.BlockSpec((tm, D), lambda m: (m, 0))
    if last:
        out_specs, out_shape = row, jax.ShapeDtypeStruct((S, D), F32)
    else:
        out_specs = [row, row]
        out_shape = [jax.ShapeDtypeStruct((S, D), F32), jax.ShapeDtypeStruct((S, D), BF)]
    return pl.pallas_call(
        functools.partial(_ffn_down_kernel, last=last),
        grid=(S // tm,),
        in_specs=[
            pl.BlockSpec((tm, D_FF), lambda m: (m, 0)),
            _resident((None, D_FF, D), lambda m: (layer, 0, 0)),
            row,
            pl.BlockSpec((1, D), lambda m: (0, 0)),
        ],
        out_specs=out_specs,
        out_shape=out_shape,
        compiler_params=_params(("parallel",)),
        name="ffn_down",
    )(a, wdown, x, g)


def _rope_lane_tables(S):
    half = ROT_DIM // 2
    inv = jnp.power(jnp.float32(ROPE_THETA), -jnp.arange(half, dtype=F32) * (2.0 / ROT_DIM))
    ang = jnp.arange(S, dtype=F32)[:, None] * inv[None, :]
    cos, sin = jnp.cos(ang), jnp.sin(ang)
    zeros = jnp.zeros((S, HEAD_DIM - ROT_DIM), F32)
    zh = jnp.zeros((S, half), F32)
    cosf = jnp.concatenate([cos, cos, jnp.ones((S, HEAD_DIM - ROT_DIM), F32)], axis=1)
    sina = jnp.concatenate([-sin, zh, zeros], axis=1)
    sinb = jnp.concatenate([zh, sin, zeros], axis=1)
    return cosf, sina, sinb


def kernel(x, norm1_g, w_in, conv_qk_w, conv_qk_b, b_igate, b_fgate, sgu_norm_g, sgu_w, sgu_b, w_branch_a, w_branch_b, w_branch_c, w_out, norm2_g, w_up, ffn_conv_w, ffn_conv_b, w_down, final_norm_g):
    B, S, D = x.shape
    depth = w_in.shape[0]
    assert B == 1 and D == D_MODEL and S % DIL_GROUPS[-1][0] == 0
    nc = S // C_CHUNK
    tm = min(1024, S)
    tm_row = min(512, S)
    tm_down = min(256, S)

    o_gate = A_COLS + B_COLS + C_MAIN
    o_g = o_gate + C_GATES
    wt = jnp.transpose(w_in, (0, 2, 1))
    w_bf = _cast_t(wt, 0, o_gate, 512)
    wgate = _cast_t(wt, o_gate, LANE, LANE)
    wg = _cast_t(wt, o_g, G_COLS, 512)
    gate_bias = jnp.pad(jnp.concatenate([b_igate, b_fgate], axis=1), ((0, 0), (0, LANE - C_GATES)))[:, None, :].astype(F32)
    pa, pb, pc = w_branch_a.astype(BF), w_branch_b.astype(BF), w_branch_c.astype(BF)
    wup, wdown = w_up.astype(BF), w_down.astype(BF)
    sgu_b_b = jnp.broadcast_to(sgu_b[:, :, :, None], sgu_b.shape + (SGU_GROUP_DIM,)).astype(F32)
    cosf, sina, sinb = _rope_lane_tables(S)

    xs = x[0]
    h = _rmsnorm(xs, norm1_g[0][None, :], tm_row)
    for l in range(depth):
        chain = None
        for g in reversed(range(N_DIL)):
            a_g = _inproj_a(h, w_bf, l, g, cosf, sina, sinb, tm)
            chain = _attn_group(a_g, g, S, chain)
        y_a = chain
        y_b = _sgu(h, w_bf, l, sgu_norm_g[l][None, :], sgu_w[l], sgu_b_b[l], tm_row)
        qk_c, v_c, og_c, gts = _inproj_c(h, w_bf, wgate, l, conv_qk_w, conv_qk_b[:, None, :], gate_bias, tm)
        gate_rows = gts[:, :C_GATES].T.reshape(C_GATES, nc, C_CHUNK)
        r, u, inter, eneg, wk, dec = _mlstm_prep(gate_rows, nc)
        colfeat = jnp.stack([u, inter, eneg, wk], axis=1).transpose(2, 3, 0, 1).reshape(S, 4 * C_HEADS)
        y_c = _mlstm(qk_c, v_c, og_c, r.transpose(1, 0, 2), dec.transpose(1, 0, 2), colfeat, S, C_SUB)
        mg = _merge(h, wg, l, y_a, y_b, y_c, pa, pb, pc, tm)
        xs, h2 = _outproj(mg, w_out, l, xs, norm2_g[:, None, :], tm_row)
        act = _ffn_up(h2, wup, l, ffn_conv_w, ffn_conv_b[:, None, :], tm)
        if l + 1 < depth:
            xs, h = _ffn_down(act, wdown, l, xs, norm1_g[l + 1][None, :], False, tm_down)
        else:
            xs = _ffn_down(act, wdown, l, xs, final_norm_g[None, :], True, tm_down)
    return xs[None]
```
